```python
import math
import jax, jax.numpy as jnp
from jax import lax
import numpy as np

D_MODEL = 1024
BATCH = 8
SEQ = 2048
DEPTH = 1
DEC_BATCH = 128
DEC_SEQ = 4
PAST_LEN = 16384
PAGE_SIZE = 128

D_POOL = D_MODEL // 4
POOL_WINDOWS = (2, 4, 8, 16)
N_POOL_GROUPS = len(POOL_WINDOWS)
POOL_GW = D_POOL // N_POOL_GROUPS
POOL_CTX = max(POOL_WINDOWS) - 1
D_RNN = D_MODEL // 2
RG_BLOCKS = 8
RG_BW = D_RNN // RG_BLOCKS
RNN_CONV = 4
RG_C = 8.0
XA_HEADS = 4
XA_HEAD_DIM = 64
D_XA = XA_HEADS * XA_HEAD_DIM
D_MIX = D_POOL + D_RNN + D_XA
D_IN = D_POOL + 2 * D_RNN + D_XA
N_MEM = 256
D_FF = 3 * D_MODEL
FFN_CONV = 3
EPS = 1e-6

kernel_name = "hybrid_pool_rglru_xattn_convffn_step"


def rms_norm(x, g):
    xf = x.astype(jnp.float32)
    y = xf * lax.rsqrt(jnp.mean(xf * xf, axis=-1, keepdims=True) + EPS)
    return (y * g.astype(jnp.float32)).astype(x.dtype)


def group_rms(x):
    xf = x.astype(jnp.float32)
    return xf * lax.rsqrt(jnp.mean(xf * xf, axis=-1, keepdims=True) + EPS)


def causal_dwconv(ctx, u, w, b):
    k = w.shape[0]
    t = u.shape[1]
    full = jnp.concatenate([ctx.astype(u.dtype), u], axis=1)
    out = b + sum(full[:, i:i + t] * w[i] for i in range(k))
    return out, full[:, t:]


def pool_mixer(ctx, u, start, w_pool, pool_scale):
    b, t, _ = u.shape
    full = jnp.concatenate([ctx.astype(u.dtype), u], axis=1)
    c = jnp.cumsum(full.astype(jnp.float32), axis=1)
    c = jnp.concatenate([jnp.zeros((b, 1, D_POOL), jnp.float32), c], axis=1)
    pos = start + jnp.arange(t)
    outs = []
    for g, w in enumerate(POOL_WINDOWS):
        cg = c[..., g * POOL_GW:(g + 1) * POOL_GW]
        s = cg[:, POOL_CTX + 1:POOL_CTX + 1 + t] - cg[:, POOL_CTX + 1 - w:POOL_CTX + 1 - w + t]
        cnt = jnp.minimum(pos + 1, w).astype(jnp.float32)[None, :, None]
        diff = s / cnt - u[..., g * POOL_GW:(g + 1) * POOL_GW].astype(jnp.float32)
        outs.append(jnp.einsum('btc,cd->btd', diff.astype(u.dtype), w_pool[g]))
    out = jnp.concatenate(outs, axis=-1) * pool_scale
    return out, full[:, t:]


def rg_lru(xc, h0, w_a, b_a, w_x, b_x, lam):
    b, t, _ = xc.shape
    xb = xc.reshape(b, t, RG_BLOCKS, RG_BW)
    r = jax.nn.sigmoid((jnp.einsum('btnc,ncd->btnd', xb, w_a).reshape(b, t, D_RNN) + b_a).astype(jnp.float32))
    i = jax.nn.sigmoid((jnp.einsum('btnc,ncd->btnd', xb, w_x).reshape(b, t, D_RNN) + b_x).astype(jnp.float32))
    log_a = -RG_C * r * jax.nn.softplus(-lam.astype(jnp.float32))
    a = jnp.exp(log_a)
    mult = jnp.sqrt(-jnp.expm1(2.0 * log_a))
    bt = mult * i * xc.astype(jnp.float32)
    bt = bt.at[:, 0].add(a[:, 0] * h0.astype(jnp.float32))

    def combine(l, rr):
        a1, b1 = l
        a2, b2 = rr
        return a1 * a2, a2 * b1 + b2

    _, h = lax.associative_scan(combine, (a, bt), axis=1)
    return h.astype(xc.dtype), h[:, -1].astype(xc.dtype)


def mem_kv(mem, g_mem, w_mk, w_mv):
    b, m, _ = mem.shape
    mn = rms_norm(mem, g_mem)
    k = (mn @ w_mk).reshape(b, m, XA_HEADS, XA_HEAD_DIM)
    v = (mn @ w_mv).reshape(b, m, XA_HEADS, XA_HEAD_DIM)
    return k, v


def cross_attn(q, k, v):
    b, t, _ = q.shape
    qh = q.reshape(b, t, XA_HEADS, XA_HEAD_DIM)
    s = jnp.einsum('bthd,bmhd->bhtm', qh, k).astype(jnp.float32) / math.sqrt(XA_HEAD_DIM)
    p = jax.nn.softmax(s, axis=-1).astype(q.dtype)
    o = jnp.einsum('bhtm,bmhd->bthd', p, v)
    return o.reshape(b, t, D_XA)


def layer(x, ctx_pool, ctx_conv, h0, ctx_ffn, mk, mv, start, p):
    xn = rms_norm(x, p['g_mix_norm'])
    proj = xn @ p['w_in']
    u_pool = proj[..., :D_POOL]
    x_rnn = proj[..., D_POOL:D_POOL + D_RNN]
    g_rnn = proj[..., D_POOL + D_RNN:D_POOL + 2 * D_RNN]
    q = proj[..., D_POOL + 2 * D_RNN:]
    o_pool, new_pool = pool_mixer(ctx_pool, u_pool, start, p['w_pool'], p['pool_scale'])
    xc, new_conv = causal_dwconv(ctx_conv, x_rnn, p['rnn_conv_w'], p['rnn_conv_b'])
    h, h_last = rg_lru(xc, h0, p['w_rg_a'], p['b_rg_a'], p['w_rg_x'], p['b_rg_x'], p['rg_lambda'])
    o_rnn = jax.nn.gelu(g_rnn, approximate=True) * h
    o_xa = cross_attn(q, mk, mv)
    mix = jnp.concatenate([group_rms(o_pool), group_rms(o_rnn), group_rms(o_xa)], axis=-1)
    mix = (mix * p['g_mix_out'].astype(jnp.float32)).astype(x.dtype)
    x = x + mix @ p['w_out']
    xn = rms_norm(x, p['g_ffn_norm'])
    gate = xn @ p['w_ff_gate']
    up = xn @ p['w_ff_up']
    gate_c, new_ffn = causal_dwconv(ctx_ffn, gate, p['ffn_conv_w'], p['ffn_conv_b'])
    x = x + (jax.nn.gelu(gate_c, approximate=True) * up) @ p['w_ff_down']
    return x, new_pool, new_conv, h_last, new_ffn


def setup_inputs(seed: int = 0) -> dict:
    key = jax.random.key(seed)
    ks = iter(jax.random.split(key, 40))
    f32 = jnp.float32

    def nrm(shape, scale):
        return jax.random.normal(next(ks), shape, f32) * scale

    def gain(shape):
        return 1.0 + nrm(shape, 0.02)

    a0 = jax.random.uniform(next(ks), (DEPTH, D_RNN), f32, 0.9, 0.999)
    a_root = a0 ** (1.0 / RG_C)
    rg_lambda = jnp.log(a_root) - jnp.log1p(-a_root)
    return {
        'x_prompt': nrm((BATCH, SEQ, D_MODEL), 1.0),
        'x_sample': nrm((DEC_BATCH, DEC_SEQ, D_MODEL), 1.0),
        'mem_prompt': nrm((BATCH, N_MEM, D_MODEL), 1.0),
        'state_pool': nrm((DEPTH, DEC_BATCH, POOL_CTX, D_POOL), 1.0),
        'state_rnn_conv': nrm((DEPTH, DEC_BATCH, RNN_CONV - 1, D_RNN), 1.0),
        'state_rnn_h': nrm((DEPTH, DEC_BATCH, D_RNN), 0.5),
        'state_ffn_conv': nrm((DEPTH, DEC_BATCH, FFN_CONV - 1, D_FF), 1.0),
        'cache_mem_k': nrm((DEPTH, DEC_BATCH, N_MEM, XA_HEADS, XA_HEAD_DIM), 1.0),
        'cache_mem_v': nrm((DEPTH, DEC_BATCH, N_MEM, XA_HEADS, XA_HEAD_DIM), 1.0),
        'g_mix_norm': gain((DEPTH, D_MODEL)),
        'w_in': nrm((DEPTH, D_MODEL, D_IN), D_MODEL ** -0.5),
        'w_pool': nrm((DEPTH, N_POOL_GROUPS, POOL_GW, POOL_GW), POOL_GW ** -0.5),
        'pool_scale': gain((DEPTH, D_POOL)),
        'rnn_conv_w': nrm((DEPTH, RNN_CONV, D_RNN), RNN_CONV ** -0.5),
        'rnn_conv_b': nrm((DEPTH, D_RNN), 0.01),
        'w_rg_a': nrm((DEPTH, RG_BLOCKS, RG_BW, RG_BW), RG_BW ** -0.5),
        'b_rg_a': nrm((DEPTH, D_RNN), 0.01),
        'w_rg_x': nrm((DEPTH, RG_BLOCKS, RG_BW, RG_BW), RG_BW ** -0.5),
        'b_rg_x': nrm((DEPTH, D_RNN), 0.01),
        'rg_lambda': rg_lambda,
        'g_mem_norm': gain((DEPTH, D_MODEL)),
        'w_mem_k': nrm((DEPTH, D_MODEL, D_XA), D_MODEL ** -0.5),
        'w_mem_v': nrm((DEPTH, D_MODEL, D_XA), D_MODEL ** -0.5),
        'g_mix_out': gain((DEPTH, D_MIX)),
        'w_out': nrm((DEPTH, D_MIX, D_MODEL), D_MIX ** -0.5),
        'g_ffn_norm': gain((DEPTH, D_MODEL)),
        'w_ff_gate': nrm((DEPTH, D_MODEL, D_FF), D_MODEL ** -0.5),
        'w_ff_up': nrm((DEPTH, D_MODEL, D_FF), D_MODEL ** -0.5),
        'ffn_conv_w': nrm((DEPTH, FFN_CONV, D_FF), FFN_CONV ** -0.5),
        'ffn_conv_b': nrm((DEPTH, D_FF), 0.01),
        'w_ff_down': nrm((DEPTH, D_FF, D_MODEL), D_FF ** -0.5),
        'g_final': gain((D_MODEL,)),
    }


def reference(x_prompt, x_sample, mem_prompt, state_pool, state_rnn_conv, state_rnn_h,
              state_ffn_conv, cache_mem_k, cache_mem_v, g_mix_norm, w_in, w_pool, pool_scale,
              rnn_conv_w, rnn_conv_b, w_rg_a, b_rg_a, w_rg_x, b_rg_x, rg_lambda, g_mem_norm,
              w_mem_k, w_mem_v, g_mix_out, w_out, g_ffn_norm, w_ff_gate, w_ff_up, ffn_conv_w,
              ffn_conv_b, w_ff_down, g_final):
    bp = x_prompt.shape[0]
    dt = x_prompt.dtype
    yp, ys = x_prompt, x_sample
    pp, pc, ph, pf, pk, pv = [], [], [], [], [], []
    sp, sc, sh, sf = [], [], [], []
    for l in range(DEPTH):
        p = {
            'g_mix_norm': g_mix_norm[l], 'w_in': w_in[l], 'w_pool': w_pool[l],
            'pool_scale': pool_scale[l], 'rnn_conv_w': rnn_conv_w[l], 'rnn_conv_b': rnn_conv_b[l],
            'w_rg_a': w_rg_a[l], 'b_rg_a': b_rg_a[l], 'w_rg_x': w_rg_x[l], 'b_rg_x': b_rg_x[l],
            'rg_lambda': rg_lambda[l], 'g_mix_out': g_mix_out[l], 'w_out': w_out[l],
            'g_ffn_norm': g_ffn_norm[l], 'w_ff_gate': w_ff_gate[l], 'w_ff_up': w_ff_up[l],
            'ffn_conv_w': ffn_conv_w[l], 'ffn_conv_b': ffn_conv_b[l], 'w_ff_down': w_ff_down[l],
        }
        mk, mv = mem_kv(mem_prompt, g_mem_norm[l], w_mem_k[l], w_mem_v[l])
        yp, n_pool, n_conv, n_h, n_ffn = layer(
            yp, jnp.zeros((bp, POOL_CTX, D_POOL), dt), jnp.zeros((bp, RNN_CONV - 1, D_RNN), dt),
            jnp.zeros((bp, D_RNN), dt), jnp.zeros((bp, FFN_CONV - 1, D_FF), dt), mk, mv, 0, p)
        pp.append(n_pool); pc.append(n_conv); ph.append(n_h); pf.append(n_ffn)
        pk.append(mk); pv.append(mv)
        ys, s_pool, s_conv, s_h, s_ffn = layer(
            ys, state_pool[l], state_rnn_conv[l], state_rnn_h[l], state_ffn_conv[l],
            cache_mem_k[l], cache_mem_v[l], PAST_LEN, p)
        sp.append(s_pool); sc.append(s_conv); sh.append(s_h); sf.append(s_ffn)
    y_prompt = rms_norm(yp, g_final)
    y_sample = rms_norm(ys, g_final)
    return (y_prompt, y_sample,
            jnp.stack(pp), jnp.stack(pc), jnp.stack(ph), jnp.stack(pf), jnp.stack(pk), jnp.stack(pv),
            jnp.stack(sp), jnp.stack(sc), jnp.stack(sh), jnp.stack(sf))
```

```python
import functools
import math

import jax
import jax.numpy as jnp
from jax import lax
from jax.experimental import pallas as pl
from jax.experimental.pallas import tpu as pltpu

D_MODEL = 1024
PAST_LEN = 16384
D_POOL = D_MODEL // 4
POOL_WINDOWS = (2, 4, 8, 16)
POOL_GW = D_POOL // len(POOL_WINDOWS)
POOL_CTX = max(POOL_WINDOWS) - 1
D_RNN = D_MODEL // 2
RG_BLOCKS = 8
RNN_CONV = 4
RG_C = 8.0
XA_HEADS = 4
XA_HEAD_DIM = 64
D_XA = XA_HEADS * XA_HEAD_DIM
D_IN = D_POOL + 2 * D_RNN + D_XA
N_MEM = 256
D_FF = 3 * D_MODEL
FFN_CONV = 3
EPS = 1e-6

SUBLANES = 8
LANES = 128
XA_LANE_HALVES = D_XA // LANES
POOL_CTX_ROWS = 16
TOKEN_TILE = 256
FF_CHUNK = 512
SEQ_CHUNK = 8
VMEM_LIMIT = 60 * 1024 * 1024

BF16 = jnp.bfloat16
F32 = jnp.float32


def _rms(x, g):
    ms = jnp.mean(x * x, axis=-1, keepdims=True)
    return x * lax.rsqrt(ms + EPS) * g


def _group_rms(x):
    ms = jnp.mean(x * x, axis=-1, keepdims=True)
    return x * lax.rsqrt(ms + EPS)


def _gelu_tanh(x):
    c = math.sqrt(2.0 / math.pi)
    return x * (0.5 * (1.0 + jnp.tanh(c * (x + 0.044715 * (x * x * x)))))


def _softplus(z):
    return jnp.maximum(z, 0.0) + jnp.log1p(jnp.exp(-jnp.abs(z)))


def _dot(a, b):
    return jnp.dot(a.astype(BF16), b, preferred_element_type=F32)


def _rg_coeffs(xc, gates, lam):
    r = jax.nn.sigmoid(gates[:, :D_RNN])
    i = jax.nn.sigmoid(gates[:, D_RNN:])
    log_a = (-RG_C) * r * _softplus(-lam)
    a = jnp.exp(log_a)
    mult = jnp.sqrt(-jnp.tanh(log_a) * (a * a + 1.0))
    return a, mult * i * xc


def _pool_select(p2, p4, p8, p16):
    lane = lax.broadcasted_iota(jnp.int32, p2.shape, 1)
    return jnp.where(lane < POOL_GW, p2,
                     jnp.where(lane < 2 * POOL_GW, p4,
                               jnp.where(lane < 3 * POOL_GW, p8, p16)))


def _pool_window_lane(shape):
    lane = lax.broadcasted_iota(jnp.int32, shape, 1)
    return jnp.where(lane < POOL_GW, POOL_WINDOWS[0],
                     jnp.where(lane < 2 * POOL_GW, POOL_WINDOWS[1],
                               jnp.where(lane < 3 * POOL_GW, POOL_WINDOWS[2], POOL_WINDOWS[3])))


def _head_mask(shape, h):
    lane = lax.broadcasted_iota(jnp.int32, shape, 1)
    return (lane >= h * XA_HEAD_DIM) & (lane < (h + 1) * XA_HEAD_DIM)


def _softmax_rows(s):
    m = jnp.max(s, axis=-1, keepdims=True)
    e = jnp.exp(s - m)
    return e / jnp.sum(e, axis=-1, keepdims=True)


def _shift_rows(ext, k, ctx_rows):
    return pltpu.roll(ext, k, 0)[ctx_rows:]


def _scan_rows(a, b):
    n = a.shape[0]
    row = lax.broadcasted_iota(jnp.int32, a.shape, 0)
    k = 1
    while k < n:
        valid = row >= k
        a_prev = jnp.where(valid, pltpu.roll(a, k, 0), 1.0)
        b_prev = jnp.where(valid, pltpu.roll(b, k, 0), 0.0)
        b = a * b_prev + b
        a = a * a_prev
        k *= 2
    return a, b


def _prompt_kernel(x_ref, mem_ref, gmem_ref, wkv_ref, g1_ref, win_ref, wpool_ref, pscale_ref,
                   cw_ref, cb_ref, wrg_ref, brg_ref, lam_ref, gout_ref, wout_ref, g2_ref,
                   wfg_ref, wfu_ref, fcw_ref, fcb_ref, wfd_ref, gfin_ref,
                   y_ref, opool_ref, oconv_ref, oh_ref, offn_ref, mk_ref, mv_ref,
                   pool_ctx, conv_ctx, h_ctx, ffn_ctx, k_scr, v_scr):
    t = pl.program_id(1)
    tt = x_ref.shape[1]

    @pl.when(t == 0)
    def _start_sequence():
        pool_ctx[...] = jnp.zeros_like(pool_ctx)
        conv_ctx[...] = jnp.zeros_like(conv_ctx)
        h_ctx[...] = jnp.zeros_like(h_ctx)
        ffn_ctx[...] = jnp.zeros_like(ffn_ctx)
        mn = _rms(mem_ref[0], gmem_ref[...])
        kv = _dot(mn, wkv_ref[...])
        mk_ref[0] = kv[:, :D_XA]
        mv_ref[0] = kv[:, D_XA:]
        k_scr[...] = kv[:, :D_XA].astype(BF16)
        v_scr[...] = kv[:, D_XA:].astype(BF16)

    x = x_ref[0]
    proj = _dot(_rms(x, g1_ref[...]), win_ref[...])
    u_pool = proj[:, :D_POOL]
    x_rnn = proj[:, D_POOL:D_POOL + D_RNN]
    g_rnn = proj[:, D_POOL + D_RNN:D_POOL + 2 * D_RNN]
    q = proj[:, D_POOL + 2 * D_RNN:]

    ext = jnp.concatenate([pool_ctx[...], u_pool], axis=0)
    p2 = ext + pltpu.roll(ext, 1, 0)
    p4 = p2 + pltpu.roll(p2, 2, 0)
    p8 = p4 + pltpu.roll(p4, 4, 0)
    p16 = p8 + pltpu.roll(p8, 8, 0)
    s = _pool_select(p2, p4, p8, p16)[POOL_CTX_ROWS:]
    pos = t * tt + lax.broadcasted_iota(jnp.int32, s.shape, 0)
    cnt = jnp.minimum(pos + 1, _pool_window_lane(s.shape)).astype(F32)
    diff = s / cnt - u_pool
    o_pool = _dot(diff, wpool_ref[...]) * pscale_ref[...]
    new_pool = ext[tt:]
    pool_ctx[...] = new_pool
    opool_ref[0] = new_pool

    ext = jnp.concatenate([conv_ctx[...], x_rnn], axis=0)
    cw = cw_ref[...]
    xc = cb_ref[...] + cw[RNN_CONV - 1:RNN_CONV] * x_rnn
    for k in range(1, RNN_CONV):
        xc = xc + cw[RNN_CONV - 1 - k:RNN_CONV - k] * _shift_rows(ext, k, SUBLANES)
    new_conv = ext[tt:]
    conv_ctx[...] = new_conv
    oconv_ref[0] = new_conv

    gates = _dot(xc, wrg_ref[...]) + brg_ref[...]
    a, bt = _rg_coeffs(xc, gates, lam_ref[...])
    a_cum, b_cum = _scan_rows(a, bt)
    h = a_cum * h_ctx[...] + b_cum
    h_last = h[tt - 1:tt]
    h_ctx[...] = h_last
    oh_ref[0] = h_last
    o_rnn = _gelu_tanh(g_rnn) * h

    kb = k_scr[...]
    vb = v_scr[...]
    o_xa = jnp.zeros((tt, D_XA), F32)
    for hd in range(XA_HEADS):
        qm = jnp.where(_head_mask(q.shape, hd), q, 0.0).astype(BF16)
        sc = lax.dot_general(qm, kb, (((1,), (1,)), ((), ())), preferred_element_type=F32)
        p = _softmax_rows(sc * (1.0 / math.sqrt(XA_HEAD_DIM)))
        pv = jnp.dot(p.astype(BF16), vb, preferred_element_type=F32)
        o_xa = o_xa + jnp.where(_head_mask(pv.shape, hd), pv, 0.0)

    mix = jnp.concatenate([_group_rms(o_pool), _group_rms(o_rnn), _group_rms(o_xa)], axis=-1)
    x1 = x + _dot(mix * gout_ref[...], wout_ref[...])

    xn2 = _rms(x1, g2_ref[...]).astype(BF16)
    acc = jnp.zeros((tt, D_MODEL), F32)
    for c in range(D_FF // FF_CHUNK):
        cols = slice(c * FF_CHUNK, (c + 1) * FF_CHUNK)
        gate = jnp.dot(xn2, wfg_ref[:, cols], preferred_element_type=F32)
        up = jnp.dot(xn2, wfu_ref[:, cols], preferred_element_type=F32)
        ext = jnp.concatenate([ffn_ctx[:, cols], gate], axis=0)
        fw = fcw_ref[:, cols]
        gate_c = fcb_ref[:, cols] + fw[FFN_CONV - 1:FFN_CONV] * gate
        for k in range(1, FFN_CONV):
            gate_c = gate_c + fw[FFN_CONV - 1 - k:FFN_CONV - k] * _shift_rows(ext, k, SUBLANES)
        new_ffn = ext[tt:]
        ffn_ctx[:, cols] = new_ffn
        offn_ref[0, :, cols] = new_ffn
        acc = acc + _dot(_gelu_tanh(gate_c) * up, wfd_ref[cols, :])
    y_ref[0] = _rms(x1 + acc, gfin_ref[...])


def _const_spec(shape):
    zeros = (0,) * len(shape)
    return pl.BlockSpec(shape, lambda *_: zeros, pipeline_mode=pl.Buffered(1))


def _prompt_layer(x, mem, w):
    batch, seq, _ = x.shape
    nt = seq // TOKEN_TILE
    consts = [w['g_mem'], w['w_kv'], w['g_mix_norm'], w['w_in'], w['w_pool'], w['pool_scale'],
              w['rnn_conv_w'], w['rnn_conv_b'], w['w_rg'], w['b_rg'], w['rg_lambda'],
              w['g_mix_out'], w['w_out'], w['g_ffn_norm'], w['w_ff_gate'], w['w_ff_up'],
              w['ffn_conv_w'], w['ffn_conv_b'], w['w_ff_down'], w['g_final']]
    per_batch = lambda b, t: (b, 0, 0)
    in_specs = [pl.BlockSpec((1, TOKEN_TILE, D_MODEL), lambda b, t: (b, t, 0)),
                pl.BlockSpec((1, N_MEM, D_MODEL), per_batch)]
    in_specs += [_const_spec(c.shape) for c in consts]
    out_shape = (jax.ShapeDtypeStruct((batch, seq, D_MODEL), F32),
                 jax.ShapeDtypeStruct((batch, POOL_CTX_ROWS, D_POOL), F32),
                 jax.ShapeDtypeStruct((batch, SUBLANES, D_RNN), F32),
                 jax.ShapeDtypeStruct((batch, 1, D_RNN), F32),
                 jax.ShapeDtypeStruct((batch, SUBLANES, D_FF), F32),
                 jax.ShapeDtypeStruct((batch, N_MEM, D_XA), F32),
                 jax.ShapeDtypeStruct((batch, N_MEM, D_XA), F32))
    out_specs = (pl.BlockSpec((1, TOKEN_TILE, D_MODEL), lambda b, t: (b, t, 0)),
                 pl.BlockSpec((1, POOL_CTX_ROWS, D_POOL), per_batch),
                 pl.BlockSpec((1, SUBLANES, D_RNN), per_batch),
                 pl.BlockSpec((1, 1, D_RNN), per_batch),
                 pl.BlockSpec((1, SUBLANES, D_FF), per_batch),
                 pl.BlockSpec((1, N_MEM, D_XA), per_batch),
                 pl.BlockSpec((1, N_MEM, D_XA), per_batch))
    scratch = [pltpu.VMEM((POOL_CTX_ROWS, D_POOL), F32),
               pltpu.VMEM((SUBLANES, D_RNN), F32),
               pltpu.VMEM((1, D_RNN), F32),
               pltpu.VMEM((SUBLANES, D_FF), F32),
               pltpu.VMEM((N_MEM, D_XA), BF16),
               pltpu.VMEM((N_MEM, D_XA), BF16)]
    return pl.pallas_call(
        _prompt_kernel,
        grid=(batch, nt),
        in_specs=in_specs,
        out_specs=out_specs,
        out_shape=out_shape,
        scratch_shapes=scratch,
        compiler_params=pltpu.CompilerParams(
            dimension_semantics=("arbitrary", "arbitrary"), vmem_limit_bytes=VMEM_LIMIT),
        name="prompt_layer",
    )(x, mem, *consts)


def _sample_mixer_kernel(xs_ref, spool_ref, sconv_ref, sh_ref, ck_ref, cv_ref,
                         g1_ref, win_ref, wpool_ref, pscale_ref, cw_ref, cb_ref,
                         wrg_ref, brg_ref, lam_ref,
                         mix_ref, npool_ref, nconv_ref, nh_ref,
                         q_scr, o_scr):
    i = pl.program_id(0)
    nb = xs_ref.shape[0]
    nt = xs_ref.shape[1] // D_MODEL

    @pl.when(i == 0)
    def _dense():
        x = jnp.concatenate([xs_ref[:, t * D_MODEL:(t + 1) * D_MODEL] for t in range(nt)], axis=0)
        proj = _dot(_rms(x, g1_ref[...]), win_ref[...])
        u_pool = proj[:, :D_POOL]
        x_rnn = proj[:, D_POOL:D_POOL + D_RNN]
        g_rnn = proj[:, D_POOL + D_RNN:D_POOL + 2 * D_RNN]
        q = proj[:, D_POOL + 2 * D_RNN:]
        rows = lambda v, t: v[t * nb:(t + 1) * nb]

        full = [spool_ref[:, j * D_POOL:(j + 1) * D_POOL] for j in range(POOL_CTX)]
        full += [rows(u_pool, t) for t in range(nt)]
        n = len(full)
        p2 = {j: full[j] + full[j - 1] for j in range(1, n)}
        p4 = {j: p2[j] + p2[j - 2] for j in range(3, n)}
        p8 = {j: p4[j] + p4[j - 4] for j in range(7, n)}
        p16 = {j: p8[j] + p8[j - 8] for j in range(15, n)}
        win = _pool_window_lane((nb, D_POOL))
        diffs = []
        for t in range(nt):
            j = POOL_CTX + t
            cnt = jnp.minimum(PAST_LEN + t + 1, win).astype(F32)
            diffs.append(_pool_select(p2[j], p4[j], p8[j], p16[j]) / cnt - full[j])
        o_pool = _dot(jnp.concatenate(diffs, axis=0), wpool_ref[...]) * pscale_ref[...]
        for j in range(POOL_CTX):
            npool_ref[:, j * D_POOL:(j + 1) * D_POOL] = full[nt + j]

        fullc = [sconv_ref[:, j * D_RNN:(j + 1) * D_RNN] for j in range(RNN_CONV - 1)]
        fullc += [rows(x_rnn, t) for t in range(nt)]
        cw = cw_ref[...]
        xcs = []
        for t in range(nt):
            acc = cb_ref[...] + cw[0:1] * fullc[t]
            for k in range(1, RNN_CONV):
                acc = acc + cw[k:k + 1] * fullc[t + k]
            xcs.append(acc)
        for j in range(RNN_CONV - 1):
            nconv_ref[:, j * D_RNN:(j + 1) * D_RNN] = fullc[nt + j]
        xc = jnp.concatenate(xcs, axis=0)
        gates = _dot(xc, wrg_ref[...]) + brg_ref[...]
        a, bt = _rg_coeffs(xc, gates, lam_ref[...])
        h = sh_ref[...]
        hs = []
        for t in range(nt):
            h = rows(a, t) * h + rows(bt, t)
            hs.append(h)
        nh_ref[...] = h
        o_rnn = _gelu_tanh(g_rnn) * jnp.concatenate(hs, axis=0)

        mix_ref[:, :D_POOL] = _group_rms(o_pool)
        mix_ref[:, D_POOL:D_POOL + D_RNN] = _group_rms(o_rnn)
        q_scr[...] = jnp.zeros_like(q_scr)
        for t in range(nt):
            for half in range(XA_LANE_HALVES):
                q_scr[half, pl.ds(t, nb, stride=SUBLANES), :] = (
                    rows(q, t)[:, half * LANES:(half + 1) * LANES])

    def _attend(j, carry):
        b = i * SEQ_CHUNK + j
        seq_rows = pl.ds(pl.multiple_of(b * SUBLANES, SUBLANES), SUBLANES)
        qb = jnp.concatenate([q_scr[half, seq_rows, :] for half in range(XA_LANE_HALVES)],
                             axis=1)
        qbd = jnp.concatenate(
            [jnp.where(_head_mask(qb.shape, hd), qb, 0.0) for hd in range(XA_HEADS)], axis=0)
        kb = ck_ref[j].astype(BF16)
        vb = cv_ref[j].astype(BF16)
        sc = lax.dot_general(qbd.astype(BF16), kb, (((1,), (1,)), ((), ())),
                             preferred_element_type=F32)
        p = _softmax_rows(sc * (1.0 / math.sqrt(XA_HEAD_DIM)))
        pv = jnp.dot(p.astype(BF16), vb, preferred_element_type=F32)
        ob = jnp.zeros((SUBLANES, D_XA), F32)
        for hd in range(XA_HEADS):
            part = pv[hd * SUBLANES:(hd + 1) * SUBLANES]
            ob = ob + jnp.where(_head_mask(part.shape, hd), part, 0.0)
        for half in range(XA_LANE_HALVES):
            o_scr[half, seq_rows, :] = ob[:, half * LANES:(half + 1) * LANES]
        return carry

    lax.fori_loop(0, SEQ_CHUNK, _attend, 0)

    @pl.when(i == pl.num_programs(0) - 1)
    def _finish():
        for t in range(nt):
            o_t = jnp.concatenate([o_scr[half, pl.ds(t, nb, stride=SUBLANES), :]
                                   for half in range(XA_LANE_HALVES)], axis=1)
            mix_ref[t * nb:(t + 1) * nb, D_POOL + D_RNN:] = _group_rms(o_t)


def _sample_mixer(xs, spool, sconv, sh, ck, cv, w):
    nb = xs.shape[0]
    nt = xs.shape[1] // D_MODEL
    consts_in = [xs, spool, sconv, sh]
    consts_w = [w['g_mix_norm'], w['w_in'], w['w_pool'], w['pool_scale'], w['rnn_conv_w'],
                w['rnn_conv_b'], w['w_rg'], w['b_rg'], w['rg_lambda']]
    kv_spec = pl.BlockSpec((SEQ_CHUNK, N_MEM, D_XA), lambda i: (i, 0, 0))
    in_specs = ([_const_spec(c.shape) for c in consts_in] + [kv_spec, kv_spec]
                + [_const_spec(c.shape) for c in consts_w])
    out_shape = (jax.ShapeDtypeStruct((nt * nb, D_MODEL), F32),
                 jax.ShapeDtypeStruct(spool.shape, F32),
                 jax.ShapeDtypeStruct(sconv.shape, F32),
                 jax.ShapeDtypeStruct(sh.shape, F32))
    out_specs = tuple(pl.BlockSpec(s.shape, lambda i, n=len(s.shape): (0,) * n) for s in out_shape)
    return pl.pallas_call(
        _sample_mixer_kernel,
        grid=(nb // SEQ_CHUNK,),
        in_specs=in_specs,
        out_specs=out_specs,
        out_shape=out_shape,
        scratch_shapes=[pltpu.VMEM((XA_LANE_HALVES, nb * SUBLANES, LANES), F32),
                        pltpu.VMEM((XA_LANE_HALVES, nb * SUBLANES, LANES), F32)],
        compiler_params=pltpu.CompilerParams(
            dimension_semantics=("arbitrary",), vmem_limit_bytes=VMEM_LIMIT),
        name="sample_mixer",
    )(*consts_in, ck, cv, *consts_w)


def _sample_ffn_kernel(xs_ref, mix_ref, sffn_ref, gout_ref, wout_ref, g2_ref, wfg_ref, wfu_ref,
                       fcw_ref, fcb_ref, wfd_ref, gfin_ref, y_ref, nffn_ref):
    nb = xs_ref.shape[0]
    nt = xs_ref.shape[1] // D_MODEL
    x = jnp.concatenate([xs_ref[:, t * D_MODEL:(t + 1) * D_MODEL] for t in range(nt)], axis=0)
    x1 = x + _dot(mix_ref[...] * gout_ref[...], wout_ref[...])
    xn2 = _rms(x1, g2_ref[...]).astype(BF16)
    acc = jnp.zeros((nt * nb, D_MODEL), F32)
    for c in range(D_FF // FF_CHUNK):
        cols = slice(c * FF_CHUNK, (c + 1) * FF_CHUNK)
        gate = jnp.dot(xn2, wfg_ref[:, cols], preferred_element_type=F32)
        up = jnp.dot(xn2, wfu_ref[:, cols], preferred_element_type=F32)
        full = [sffn_ref[:, j * D_FF + c * FF_CHUNK:j * D_FF + (c + 1) * FF_CHUNK]
                for j in range(FFN_CONV - 1)]
        full += [gate[t * nb:(t + 1) * nb] for t in range(nt)]
        fw = fcw_ref[:, cols]
        convs = []
        for t in range(nt):
            a = fcb_ref[:, cols] + fw[0:1] * full[t]
            for k in range(1, FFN_CONV):
                a = a + fw[k:k + 1] * full[t + k]
            convs.append(a)
        for j in range(FFN_CONV - 1):
            nffn_ref[:, j * D_FF + c * FF_CHUNK:j * D_FF + (c + 1) * FF_CHUNK] = full[nt + j]
        gate_c = jnp.concatenate(convs, axis=0)
        acc = acc + _dot(_gelu_tanh(gate_c) * up, wfd_ref[cols, :])
    y = _rms(x1 + acc, gfin_ref[...])
    for t in range(nt):
        y_ref[:, t * D_MODEL:(t + 1) * D_MODEL] = y[t * nb:(t + 1) * nb]


def _sample_ffn(xs, mix, sffn, w):
    ins = [xs, mix, sffn, w['g_mix_out'], w['w_out'], w['g_ffn_norm'], w['w_ff_gate'],
           w['w_ff_up'], w['ffn_conv_w'], w['ffn_conv_b'], w['w_ff_down'], w['g_final']]
    out_shape = (jax.ShapeDtypeStruct(xs.shape, F32), jax.ShapeDtypeStruct(sffn.shape, F32))
    return pl.pallas_call(
        _sample_ffn_kernel,
        grid=(1,),
        in_specs=[_const_spec(c.shape) for c in ins],
        out_specs=tuple(pl.BlockSpec(s.shape, lambda i: (0, 0)) for s in out_shape),
        out_shape=out_shape,
        compiler_params=pltpu.CompilerParams(
            dimension_semantics=("arbitrary",), vmem_limit_bytes=VMEM_LIMIT),
        name="sample_ffn",
    )(*ins)


def _block_diag(blocks):
    n, c, d = blocks.shape
    eye = jnp.eye(n, dtype=blocks.dtype)
    return (eye[:, None, :, None] * blocks[:, :, None, :]).reshape(n * c, n * d)


def _layer_weights(l, g_mix_norm, w_in, w_pool, pool_scale, rnn_conv_w, rnn_conv_b, w_rg_a,
                   b_rg_a, w_rg_x, b_rg_x, rg_lambda, g_mem_norm, w_mem_k, w_mem_v, g_mix_out,
                   w_out, g_ffn_norm, w_ff_gate, w_ff_up, ffn_conv_w, ffn_conv_b, w_ff_down,
                   g_final):
    row = lambda v: v.reshape(1, -1)
    return {
        'g_mix_norm': row(g_mix_norm[l]),
        'w_in': w_in[l].astype(BF16),
        'w_pool': _block_diag(w_pool[l]).astype(BF16),
        'pool_scale': row(pool_scale[l]),
        'rnn_conv_w': rnn_conv_w[l],
        'rnn_conv_b': row(rnn_conv_b[l]),
        'w_rg': jnp.concatenate([_block_diag(w_rg_a[l]), _block_diag(w_rg_x[l])], axis=1).astype(BF16),
        'b_rg': jnp.concatenate([b_rg_a[l], b_rg_x[l]]).reshape(1, -1),
        'rg_lambda': row(rg_lambda[l]),
        'g_mem': row(g_mem_norm[l]),
        'w_kv': jnp.concatenate([w_mem_k[l], w_mem_v[l]], axis=1).astype(BF16),
        'g_mix_out': row(g_mix_out[l]),
        'w_out': w_out[l].astype(BF16),
        'g_ffn_norm': row(g_ffn_norm[l]),
        'w_ff_gate': w_ff_gate[l].astype(BF16),
        'w_ff_up': w_ff_up[l].astype(BF16),
        'ffn_conv_w': ffn_conv_w[l],
        'ffn_conv_b': row(ffn_conv_b[l]),
        'w_ff_down': w_ff_down[l].astype(BF16),
        'g_final': row(g_final),
    }


def kernel(x_prompt, x_sample, mem_prompt, state_pool, state_rnn_conv, state_rnn_h, state_ffn_conv, cache_mem_k, cache_mem_v, g_mix_norm, w_in, w_pool, pool_scale, rnn_conv_w, rnn_conv_b, w_rg_a, b_rg_a, w_rg_x, b_rg_x, rg_lambda, g_mem_norm, w_mem_k, w_mem_v, g_mix_out, w_out, g_ffn_norm, w_ff_gate, w_ff_up, ffn_conv_w, ffn_conv_b, w_ff_down, g_final):
    depth = w_in.shape[0]
    assert depth == 1, "the final norm is fused into the single layer"
    bp = x_prompt.shape[0]
    nb, nt, _ = x_sample.shape
    w = _layer_weights(0, g_mix_norm, w_in, w_pool, pool_scale, rnn_conv_w, rnn_conv_b, w_rg_a,
                       b_rg_a, w_rg_x, b_rg_x, rg_lambda, g_mem_norm, w_mem_k, w_mem_v,
                       g_mix_out, w_out, g_ffn_norm, w_ff_gate, w_ff_up, ffn_conv_w, ffn_conv_b,
                       w_ff_down, g_final)

    y_p, pool_p, conv_p, h_p, ffn_p, mk, mv = _prompt_layer(x_prompt, mem_prompt, w)

    xs = x_sample.reshape(nb, nt * D_MODEL)
    spool = state_pool[0].reshape(nb, POOL_CTX * D_POOL)
    sconv = state_rnn_conv[0].reshape(nb, (RNN_CONV - 1) * D_RNN)
    sffn = state_ffn_conv[0].reshape(nb, (FFN_CONV - 1) * D_FF)
    ck = cache_mem_k[0].reshape(nb, N_MEM, D_XA)
    cv = cache_mem_v[0].reshape(nb, N_MEM, D_XA)
    mix, pool_s, conv_s, h_s = _sample_mixer(xs, spool, sconv, state_rnn_h[0], ck, cv, w)
    y_s, ffn_s = _sample_ffn(xs, mix, sffn, w)

    return (y_p,
            y_s.reshape(nb, nt, D_MODEL),
            pool_p[:, POOL_CTX_ROWS - POOL_CTX:][None],
            conv_p[:, SUBLANES - (RNN_CONV - 1):][None],
            h_p.reshape(1, bp, D_RNN),
            ffn_p[:, SUBLANES - (FFN_CONV - 1):][None],
            mk.reshape(1, bp, N_MEM, XA_HEADS, XA_HEAD_DIM),
            mv.reshape(1, bp, N_MEM, XA_HEADS, XA_HEAD_DIM),
            pool_s.reshape(1, nb, POOL_CTX, D_POOL),
            conv_s.reshape(1, nb, RNN_CONV - 1, D_RNN),
            h_s[None],
            ffn_s.reshape(1, nb, FFN_CONV - 1, D_FF))
```

```python
import math

import jax
import jax.numpy as jnp
from jax import lax
from jax.experimental import pallas as pl
from jax.experimental.pallas import tpu as pltpu

D_MODEL = 1024
PAST_LEN = 16384
D_POOL = D_MODEL // 4
POOL_WINDOWS = (2, 4, 8, 16)
POOL_GW = D_POOL // len(POOL_WINDOWS)
POOL_CTX = max(POOL_WINDOWS) - 1
D_RNN = D_MODEL // 2
RG_BLOCKS = 8
RNN_CONV = 4
RG_C = 8.0
XA_HEADS = 4
XA_HEAD_DIM = 64
D_XA = XA_HEADS * XA_HEAD_DIM
D_IN = D_POOL + 2 * D_RNN + D_XA
N_MEM = 256
D_FF = 3 * D_MODEL
FFN_CONV = 3
EPS = 1e-6

SUBLANES = 8
LANES = 128
MXU_DIM = 256
XA_LANE_HALVES = D_XA // LANES
RG_TILES = D_RNN // MXU_DIM
POOL_CTX_ROWS = 16
TOKEN_TILE = 512
FF_CHUNK = 512
FF_CHUNKS = D_FF // FF_CHUNK
SEQ_CHUNK = 8
VMEM_LIMIT = 60 * 1024 * 1024

BF16 = jnp.bfloat16
F32 = jnp.float32


def _rms(x, g):
    ms = jnp.mean(x * x, axis=-1, keepdims=True)
    return x * lax.rsqrt(ms + EPS) * g


def _group_rms(x):
    ms = jnp.mean(x * x, axis=-1, keepdims=True)
    return x * lax.rsqrt(ms + EPS)


def _gelu_tanh(x):
    c = math.sqrt(2.0 / math.pi)
    half = 0.5 * x
    return half * jnp.tanh(x * (c + (c * 0.044715) * (x * x))) + half


def _softplus(z):
    return jnp.maximum(z, 0.0) + jnp.log1p(jnp.exp(-jnp.abs(z)))


def _dot(a, b):
    return jnp.dot(a.astype(BF16), b, preferred_element_type=F32)


def _gate_dots(xc, wrg_ref, brg_ref):
    out = []
    for gate in range(2):
        parts = [_dot(xc[:, j * MXU_DIM:(j + 1) * MXU_DIM], wrg_ref[gate, j])
                 for j in range(RG_TILES)]
        out.append(jnp.concatenate(parts, axis=1) + brg_ref[gate:gate + 1])
    return out


def _rg_coeffs(xc, r_pre, i_pre, lam):
    r = jax.nn.sigmoid(r_pre)
    i = jax.nn.sigmoid(i_pre)
    log_a = (-RG_C) * r * _softplus(-lam)
    a = jnp.exp(log_a)
    mult = jnp.sqrt(-jnp.tanh(log_a) * (a * a + 1.0))
    return a, mult * i * xc


def _pool_select(p2, p4, p8, p16):
    lane = lax.broadcasted_iota(jnp.int32, p2.shape, 1)
    return jnp.where(lane < POOL_GW, p2,
                     jnp.where(lane < 2 * POOL_GW, p4,
                               jnp.where(lane < 3 * POOL_GW, p8, p16)))


def _pool_window_lane(shape):
    lane = lax.broadcasted_iota(jnp.int32, shape, 1)
    return jnp.where(lane < POOL_GW, POOL_WINDOWS[0],
                     jnp.where(lane < 2 * POOL_GW, POOL_WINDOWS[1],
                               jnp.where(lane < 3 * POOL_GW, POOL_WINDOWS[2], POOL_WINDOWS[3])))


def _head_mask(shape, h):
    lane = lax.broadcasted_iota(jnp.int32, shape, 1)
    return (lane >= h * XA_HEAD_DIM) & (lane < (h + 1) * XA_HEAD_DIM)


def _softmax_rows(s):
    m = jnp.max(s, axis=-1, keepdims=True)
    e = jnp.exp(s - m)
    return e / jnp.sum(e, axis=-1, keepdims=True)


def _shift_rows(ext, k, ctx_rows):
    return pltpu.roll(ext, k, 0)[ctx_rows:]


def _scan_rows(a, b, h_in):
    n, c = a.shape
    groups = n // SUBLANES
    a3 = a.reshape(groups, SUBLANES, c)
    b3 = b.reshape(groups, SUBLANES, c)
    sub = lax.broadcasted_iota(jnp.int32, (1, SUBLANES, c), 1)
    k = 1
    while k < SUBLANES:
        valid = sub >= k
        a_prev = jnp.where(valid, pltpu.roll(a3, k, 1), 1.0)
        b_prev = jnp.where(valid, pltpu.roll(b3, k, 1), 0.0)
        b3 = a3 * b_prev + b3
        a3 = a3 * a_prev
        k *= 2
    carry = h_in
    hs = []
    for g in range(groups):
        h = a3[g] * carry + b3[g]
        hs.append(h)
        carry = h[SUBLANES - 1:SUBLANES]
    return jnp.concatenate(hs, axis=0)


def _prompt_kernel(x_ref, mem_ref, gmem_ref, wkv_ref, g1_ref, win_ref, wpool_ref, pscale_ref,
                   cw_ref, cb_ref, wrg_ref, brg_ref, lam_ref, gout_ref, wout_ref, g2_ref,
                   wfg_ref, wfu_ref, fcw_ref, fcb_ref, wfd_ref, gfin_ref,
                   y_ref, opool_ref, oconv_ref, oh_ref, offn_ref, mk_ref, mv_ref,
                   pool_ctx, conv_ctx, h_ctx, ffn_ctx, kt_scr, v_scr, act_scr):
    t = pl.program_id(1)
    tt = x_ref.shape[1]

    @pl.when(t == 0)
    def _start_sequence():
        pool_ctx[...] = jnp.zeros_like(pool_ctx)
        conv_ctx[...] = jnp.zeros_like(conv_ctx)
        h_ctx[...] = jnp.zeros_like(h_ctx)
        ffn_ctx[...] = jnp.zeros_like(ffn_ctx)
        mn = _rms(mem_ref[0], gmem_ref[...])
        kv = _dot(mn, wkv_ref[...])
        kt = kv[:, :D_XA].T
        mk_ref[0] = kt
        mv_ref[0] = kv[:, D_XA:].T
        kt_scr[...] = kt.astype(BF16)
        v_scr[...] = kv[:, D_XA:].astype(BF16)

    x = x_ref[0]
    proj = _dot(_rms(x, g1_ref[...]), win_ref[...])
    u_pool = proj[:, :D_POOL]
    x_rnn = proj[:, D_POOL:D_POOL + D_RNN]
    g_rnn = proj[:, D_POOL + D_RNN:D_POOL + 2 * D_RNN]
    q = proj[:, D_POOL + 2 * D_RNN:]

    ext = jnp.concatenate([pool_ctx[...], u_pool], axis=0)
    p2 = ext + pltpu.roll(ext, 1, 0)
    p4 = p2 + pltpu.roll(p2, 2, 0)
    p8 = p4 + pltpu.roll(p4, 4, 0)
    p16 = p8 + pltpu.roll(p8, 8, 0)
    s = _pool_select(p2, p4, p8, p16)[POOL_CTX_ROWS:]
    pos = t * tt + lax.broadcasted_iota(jnp.int32, s.shape, 0)
    cnt = jnp.minimum(pos + 1, _pool_window_lane(s.shape)).astype(F32)
    diff = s / cnt - u_pool
    o_pool = _dot(diff, wpool_ref[...]) * pscale_ref[...]
    new_pool = ext[tt:]
    pool_ctx[...] = new_pool
    opool_ref[0] = new_pool

    ext = jnp.concatenate([conv_ctx[...], x_rnn], axis=0)
    cw = cw_ref[...]
    xc = cb_ref[...] + cw[RNN_CONV - 1:RNN_CONV] * x_rnn
    for k in range(1, RNN_CONV):
        xc = xc + cw[RNN_CONV - 1 - k:RNN_CONV - k] * _shift_rows(ext, k, SUBLANES)
    new_conv = ext[tt:]
    conv_ctx[...] = new_conv
    oconv_ref[0] = new_conv

    r_pre, i_pre = _gate_dots(xc, wrg_ref, brg_ref)
    a, bt = _rg_coeffs(xc, r_pre, i_pre, lam_ref[...])
    h = _scan_rows(a, bt, h_ctx[...])
    h_last = h[tt - 1:tt]
    h_ctx[...] = h_last
    oh_ref[0] = h_last
    o_rnn = _gelu_tanh(g_rnn) * h

    ktb = kt_scr[...]
    vb = v_scr[...]
    qs = q * (1.0 / math.sqrt(XA_HEAD_DIM))
    o_xa = jnp.zeros((tt, D_XA), F32)
    for hd in range(XA_HEADS):
        qm = jnp.where(_head_mask(qs.shape, hd), qs, 0.0).astype(BF16)
        p = _softmax_rows(jnp.dot(qm, ktb, preferred_element_type=F32))
        pv = jnp.dot(p.astype(BF16), vb, preferred_element_type=F32)
        o_xa = jnp.where(_head_mask(pv.shape, hd), pv, o_xa)

    mix = jnp.concatenate([_group_rms(o_pool), _group_rms(o_rnn), _group_rms(o_xa)], axis=-1)
    x1 = x + _dot(mix * gout_ref[...], wout_ref[...])

    xn2 = _rms(x1, g2_ref[...]).astype(BF16)
    for c in range(FF_CHUNKS):
        cols = slice(c * FF_CHUNK, (c + 1) * FF_CHUNK)
        gate = jnp.dot(xn2, wfg_ref[c], preferred_element_type=F32)
        up = jnp.dot(xn2, wfu_ref[c], preferred_element_type=F32)
        ext = jnp.concatenate([ffn_ctx[:, cols], gate], axis=0)
        fw = fcw_ref[:, cols]
        gate_c = fcb_ref[:, cols] + fw[FFN_CONV - 1:FFN_CONV] * gate
        for k in range(1, FFN_CONV):
            gate_c = gate_c + fw[FFN_CONV - 1 - k:FFN_CONV - k] * _shift_rows(ext, k, SUBLANES)
        new_ffn = ext[tt:]
        ffn_ctx[:, cols] = new_ffn
        offn_ref[0, :, cols] = new_ffn
        act_scr[:, cols] = (_gelu_tanh(gate_c) * up).astype(BF16)
    x2 = x1 + jnp.dot(act_scr[...], wfd_ref[...], preferred_element_type=F32)
    y_ref[0] = _rms(x2, gfin_ref[...])


def _const_spec(shape):
    zeros = (0,) * len(shape)
    return pl.BlockSpec(shape, lambda *_: zeros, pipeline_mode=pl.Buffered(1))


def _prompt_layer(x, mem, w):
    batch, seq, _ = x.shape
    nt = seq // TOKEN_TILE
    consts = [w['g_mem'], w['w_kv'], w['g_mix_norm'], w['w_in'], w['w_pool'], w['pool_scale'],
              w['rnn_conv_w'], w['rnn_conv_b'], w['w_rg'], w['b_rg'], w['rg_lambda'],
              w['g_mix_out'], w['w_out'], w['g_ffn_norm'], w['w_ff_gate'], w['w_ff_up'],
              w['ffn_conv_w'], w['ffn_conv_b'], w['w_ff_down'], w['g_final']]
    per_batch = lambda b, t: (b, 0, 0)
    in_specs = [pl.BlockSpec((1, TOKEN_TILE, D_MODEL), lambda b, t: (b, t, 0)),
                pl.BlockSpec((1, N_MEM, D_MODEL), per_batch, pipeline_mode=pl.Buffered(1))]
    in_specs += [_const_spec(c.shape) for c in consts]
    out_shape = (jax.ShapeDtypeStruct((batch, seq, D_MODEL), F32),
                 jax.ShapeDtypeStruct((batch, POOL_CTX_ROWS, D_POOL), F32),
                 jax.ShapeDtypeStruct((batch, SUBLANES, D_RNN), F32),
                 jax.ShapeDtypeStruct((batch, 1, D_RNN), F32),
                 jax.ShapeDtypeStruct((batch, SUBLANES, D_FF), F32),
                 jax.ShapeDtypeStruct((batch, D_XA, N_MEM), F32),
                 jax.ShapeDtypeStruct((batch, D_XA, N_MEM), F32))
    out_specs = (pl.BlockSpec((1, TOKEN_TILE, D_MODEL), lambda b, t: (b, t, 0)),
                 pl.BlockSpec((1, POOL_CTX_ROWS, D_POOL), per_batch),
                 pl.BlockSpec((1, SUBLANES, D_RNN), per_batch),
                 pl.BlockSpec((1, 1, D_RNN), per_batch),
                 pl.BlockSpec((1, SUBLANES, D_FF), per_batch),
                 pl.BlockSpec((1, D_XA, N_MEM), per_batch),
                 pl.BlockSpec((1, D_XA, N_MEM), per_batch))
    scratch = [pltpu.VMEM((POOL_CTX_ROWS, D_POOL), F32),
               pltpu.VMEM((SUBLANES, D_RNN), F32),
               pltpu.VMEM((1, D_RNN), F32),
               pltpu.VMEM((SUBLANES, D_FF), F32),
               pltpu.VMEM((D_XA, N_MEM), BF16),
               pltpu.VMEM((N_MEM, D_XA), BF16),
               pltpu.VMEM((TOKEN_TILE, D_FF), BF16)]
    return pl.pallas_call(
        _prompt_kernel,
        grid=(batch, nt),
        in_specs=in_specs,
        out_specs=out_specs,
        out_shape=out_shape,
        scratch_shapes=scratch,
        compiler_params=pltpu.CompilerParams(
            dimension_semantics=("arbitrary", "arbitrary"), vmem_limit_bytes=VMEM_LIMIT),
        name="prompt_layer",
    )(x, mem, *consts)


def _sample_mixer_kernel(xs_ref, spool_ref, sconv_ref, sh_ref, ckt_ref, cvt_ref,
                         g1_ref, win_ref, wpool_ref, pscale_ref, cw_ref, cb_ref,
                         wrg_ref, brg_ref, lam_ref,
                         mix_ref, npool_ref, nconv_ref, nh_ref,
                         q_scr, o_scr):
    i = pl.program_id(0)
    nb = xs_ref.shape[0]
    nt = xs_ref.shape[1] // D_MODEL

    @pl.when(i == 0)
    def _dense():
        x = jnp.concatenate([xs_ref[:, t * D_MODEL:(t + 1) * D_MODEL] for t in range(nt)], axis=0)
        proj = _dot(_rms(x, g1_ref[...]), win_ref[...])
        u_pool = proj[:, :D_POOL]
        x_rnn = proj[:, D_POOL:D_POOL + D_RNN]
        g_rnn = proj[:, D_POOL + D_RNN:D_POOL + 2 * D_RNN]
        q = proj[:, D_POOL + 2 * D_RNN:] * (1.0 / math.sqrt(XA_HEAD_DIM))
        rows = lambda v, t: v[t * nb:(t + 1) * nb]

        full = [spool_ref[j] for j in range(POOL_CTX)] + [rows(u_pool, t) for t in range(nt)]
        n = len(full)
        p2 = {j: full[j] + full[j - 1] for j in range(1, n)}
        p4 = {j: p2[j] + p2[j - 2] for j in range(3, n)}
        p8 = {j: p4[j] + p4[j - 4] for j in range(7, n)}
        p16 = {j: p8[j] + p8[j - 8] for j in range(15, n)}
        win = _pool_window_lane((nb, D_POOL))
        diffs = []
        for t in range(nt):
            j = POOL_CTX + t
            cnt = jnp.minimum(PAST_LEN + t + 1, win).astype(F32)
            diffs.append(_pool_select(p2[j], p4[j], p8[j], p16[j]) / cnt - full[j])
        o_pool = _dot(jnp.concatenate(diffs, axis=0), wpool_ref[...]) * pscale_ref[...]
        for j in range(POOL_CTX):
            npool_ref[j] = full[nt + j]

        fullc = [sconv_ref[j] for j in range(RNN_CONV - 1)] + [rows(x_rnn, t) for t in range(nt)]
        cw = cw_ref[...]
        xcs = []
        for t in range(nt):
            acc = cb_ref[...] + cw[0:1] * fullc[t]
            for k in range(1, RNN_CONV):
                acc = acc + cw[k:k + 1] * fullc[t + k]
            xcs.append(acc)
        for j in range(RNN_CONV - 1):
            nconv_ref[j] = fullc[nt + j]
        xc = jnp.concatenate(xcs, axis=0)
        r_pre, i_pre = _gate_dots(xc, wrg_ref, brg_ref)
        a, bt = _rg_coeffs(xc, r_pre, i_pre, lam_ref[...])
        h = sh_ref[...]
        hs = []
        for t in range(nt):
            h = rows(a, t) * h + rows(bt, t)
            hs.append(h)
        nh_ref[...] = h
        o_rnn = _gelu_tanh(g_rnn) * jnp.concatenate(hs, axis=0)

        mix_ref[:, :D_POOL] = _group_rms(o_pool)
        mix_ref[:, D_POOL:D_POOL + D_RNN] = _group_rms(o_rnn)
        q_scr[...] = jnp.zeros_like(q_scr)
        for t in range(nt):
            for half in range(XA_LANE_HALVES):
                q_scr[half, pl.ds(t, nb, stride=SUBLANES), :] = (
                    rows(q, t)[:, half * LANES:(half + 1) * LANES])

    for j in range(SEQ_CHUNK):
        seq_rows = pl.ds(pl.multiple_of((i * SEQ_CHUNK + j) * SUBLANES, SUBLANES), SUBLANES)
        qb = jnp.concatenate([q_scr[half, seq_rows, :] for half in range(XA_LANE_HALVES)],
                             axis=1)
        qbd = jnp.concatenate(
            [jnp.where(_head_mask(qb.shape, hd), qb, 0.0) for hd in range(XA_HEADS)], axis=0)
        ktb = ckt_ref[j].astype(BF16)
        vtb = cvt_ref[j].astype(BF16)
        p = _softmax_rows(jnp.dot(qbd.astype(BF16), ktb, preferred_element_type=F32))
        pv = lax.dot_general(p.astype(BF16), vtb, (((1,), (1,)), ((), ())),
                             preferred_element_type=F32)
        ob = jnp.zeros((SUBLANES, D_XA), F32)
        for hd in range(XA_HEADS):
            part = pv[hd * SUBLANES:(hd + 1) * SUBLANES]
            ob = jnp.where(_head_mask(part.shape, hd), part, ob)
        for half in range(XA_LANE_HALVES):
            o_scr[half, seq_rows, :] = ob[:, half * LANES:(half + 1) * LANES]

    @pl.when(i == pl.num_programs(0) - 1)
    def _finish():
        for t in range(nt):
            o_t = jnp.concatenate([o_scr[half, pl.ds(t, nb, stride=SUBLANES), :]
                                   for half in range(XA_LANE_HALVES)], axis=1)
            mix_ref[t * nb:(t + 1) * nb, D_POOL + D_RNN:] = _group_rms(o_t)


def _sample_mixer(xs, spool, sconv, sh, ckt, cvt, w):
    nb = xs.shape[0]
    nt = xs.shape[1] // D_MODEL
    consts_in = [xs, spool, sconv, sh]
    consts_w = [w['g_mix_norm'], w['w_in'], w['w_pool'], w['pool_scale'], w['rnn_conv_w'],
                w['rnn_conv_b'], w['w_rg'], w['b_rg'], w['rg_lambda']]
    kv_spec = pl.BlockSpec((SEQ_CHUNK, D_XA, N_MEM), lambda i: (i, 0, 0))
    in_specs = ([_const_spec(c.shape) for c in consts_in] + [kv_spec, kv_spec]
                + [_const_spec(c.shape) for c in consts_w])
    out_shape = (jax.ShapeDtypeStruct((nt * nb, D_MODEL), F32),
                 jax.ShapeDtypeStruct(spool.shape, F32),
                 jax.ShapeDtypeStruct(sconv.shape, F32),
                 jax.ShapeDtypeStruct(sh.shape, F32))
    out_specs = tuple(pl.BlockSpec(s.shape, lambda i, n=len(s.shape): (0,) * n) for s in out_shape)
    return pl.pallas_call(
        _sample_mixer_kernel,
        grid=(nb // SEQ_CHUNK,),
        in_specs=in_specs,
        out_specs=out_specs,
        out_shape=out_shape,
        scratch_shapes=[pltpu.VMEM((XA_LANE_HALVES, nb * SUBLANES, LANES), F32),
                        pltpu.VMEM((XA_LANE_HALVES, nb * SUBLANES, LANES), F32)],
        compiler_params=pltpu.CompilerParams(
            dimension_semantics=("arbitrary",), vmem_limit_bytes=VMEM_LIMIT),
        name="sample_mixer",
    )(*consts_in, ckt, cvt, *consts_w)


def _sample_ffn_kernel(xs_ref, mix_ref, sffn_ref, gout_ref, wout_ref, g2_ref, wfg_ref, wfu_ref,
                       fcw_ref, fcb_ref, wfd_ref, gfin_ref, y_ref, nffn_ref, act_scr):
    nb = xs_ref.shape[0]
    nt = xs_ref.shape[1] // D_MODEL
    x = jnp.concatenate([xs_ref[:, t * D_MODEL:(t + 1) * D_MODEL] for t in range(nt)], axis=0)
    x1 = x + _dot(mix_ref[...] * gout_ref[...], wout_ref[...])
    xn2 = _rms(x1, g2_ref[...]).astype(BF16)
    for c in range(FF_CHUNKS):
        cols = slice(c * FF_CHUNK, (c + 1) * FF_CHUNK)
        gate = jnp.dot(xn2, wfg_ref[c], preferred_element_type=F32)
        up = jnp.dot(xn2, wfu_ref[c], preferred_element_type=F32)
        full = [sffn_ref[:, j * D_FF + c * FF_CHUNK:j * D_FF + (c + 1) * FF_CHUNK]
                for j in range(FFN_CONV - 1)]
        full += [gate[t * nb:(t + 1) * nb] for t in range(nt)]
        fw = fcw_ref[:, cols]
        convs = []
        for t in range(nt):
            a = fcb_ref[:, cols] + fw[0:1] * full[t]
            for k in range(1, FFN_CONV):
                a = a + fw[k:k + 1] * full[t + k]
            convs.append(a)
        for j in range(FFN_CONV - 1):
            nffn_ref[:, j * D_FF + c * FF_CHUNK:j * D_FF + (c + 1) * FF_CHUNK] = full[nt + j]
        gate_c = jnp.concatenate(convs, axis=0)
        act_scr[:, cols] = (_gelu_tanh(gate_c) * up).astype(BF16)
    x2 = x1 + jnp.dot(act_scr[...], wfd_ref[...], preferred_element_type=F32)
    y = _rms(x2, gfin_ref[...])
    for t in range(nt):
        y_ref[:, t * D_MODEL:(t + 1) * D_MODEL] = y[t * nb:(t + 1) * nb]


def _sample_ffn(xs, mix, sffn, w):
    ins = [xs, mix, sffn, w['g_mix_out'], w['w_out'], w['g_ffn_norm'], w['w_ff_gate'],
           w['w_ff_up'], w['ffn_conv_w'], w['ffn_conv_b'], w['w_ff_down'], w['g_final']]
    out_shape = (jax.ShapeDtypeStruct(xs.shape, F32), jax.ShapeDtypeStruct(sffn.shape, F32))
    return pl.pallas_call(
        _sample_ffn_kernel,
        grid=(1,),
        in_specs=[_const_spec(c.shape) for c in ins],
        out_specs=tuple(pl.BlockSpec(s.shape, lambda i: (0, 0)) for s in out_shape),
        out_shape=out_shape,
        scratch_shapes=[pltpu.VMEM((mix.shape[0], D_FF), BF16)],
        compiler_params=pltpu.CompilerParams(
            dimension_semantics=("arbitrary",), vmem_limit_bytes=VMEM_LIMIT),
        name="sample_ffn",
    )(*ins)


def _block_diag(blocks):
    n, c, d = blocks.shape
    eye = jnp.eye(n, dtype=blocks.dtype)
    return (eye[:, None, :, None] * blocks[:, :, None, :]).reshape(n * c, n * d)


def _diag_tiles(blocks):
    bd = _block_diag(blocks)
    return jnp.stack([bd[j * MXU_DIM:(j + 1) * MXU_DIM, j * MXU_DIM:(j + 1) * MXU_DIM]
                      for j in range(bd.shape[0] // MXU_DIM)])


def _column_chunks(wmat):
    k, n = wmat.shape
    return wmat.reshape(k, n // FF_CHUNK, FF_CHUNK).transpose(1, 0, 2)


def _layer_weights(l, g_mix_norm, w_in, w_pool, pool_scale, rnn_conv_w, rnn_conv_b, w_rg_a,
                   b_rg_a, w_rg_x, b_rg_x, rg_lambda, g_mem_norm, w_mem_k, w_mem_v, g_mix_out,
                   w_out, g_ffn_norm, w_ff_gate, w_ff_up, ffn_conv_w, ffn_conv_b, w_ff_down,
                   g_final):
    row = lambda v: v.reshape(1, -1)
    return {
        'g_mix_norm': row(g_mix_norm[l]),
        'w_in': w_in[l].astype(BF16),
        'w_pool': _block_diag(w_pool[l]).astype(BF16),
        'pool_scale': row(pool_scale[l]),
        'rnn_conv_w': rnn_conv_w[l],
        'rnn_conv_b': row(rnn_conv_b[l]),
        'w_rg': jnp.stack([_diag_tiles(w_rg_a[l]), _diag_tiles(w_rg_x[l])]).astype(BF16),
        'b_rg': jnp.stack([b_rg_a[l], b_rg_x[l]]),
        'rg_lambda': row(rg_lambda[l]),
        'g_mem': row(g_mem_norm[l]),
        'w_kv': jnp.concatenate([w_mem_k[l], w_mem_v[l]], axis=1).astype(BF16),
        'g_mix_out': row(g_mix_out[l]),
        'w_out': w_out[l].astype(BF16),
        'g_ffn_norm': row(g_ffn_norm[l]),
        'w_ff_gate': _column_chunks(w_ff_gate[l].astype(BF16)),
        'w_ff_up': _column_chunks(w_ff_up[l].astype(BF16)),
        'ffn_conv_w': ffn_conv_w[l],
        'ffn_conv_b': row(ffn_conv_b[l]),
        'w_ff_down': w_ff_down[l].astype(BF16),
        'g_final': row(g_final),
    }


def _mem_major(cache):
    b = cache.shape[0]
    return jnp.transpose(cache, (0, 2, 3, 1)).reshape(b, D_XA, N_MEM)


def _mem_minor(kt):
    b = kt.shape[0]
    return jnp.transpose(kt.reshape(b, XA_HEADS, XA_HEAD_DIM, N_MEM), (0, 3, 1, 2))


def kernel(x_prompt, x_sample, mem_prompt, state_pool, state_rnn_conv, state_rnn_h, state_ffn_conv, cache_mem_k, cache_mem_v, g_mix_norm, w_in, w_pool, pool_scale, rnn_conv_w, rnn_conv_b, w_rg_a, b_rg_a, w_rg_x, b_rg_x, rg_lambda, g_mem_norm, w_mem_k, w_mem_v, g_mix_out, w_out, g_ffn_norm, w_ff_gate, w_ff_up, ffn_conv_w, ffn_conv_b, w_ff_down, g_final):
    depth = w_in.shape[0]
    assert depth == 1, "the final norm is fused into the single layer"
    bp = x_prompt.shape[0]
    nb, nt, _ = x_sample.shape
    w = _layer_weights(0, g_mix_norm, w_in, w_pool, pool_scale, rnn_conv_w, rnn_conv_b, w_rg_a,
                       b_rg_a, w_rg_x, b_rg_x, rg_lambda, g_mem_norm, w_mem_k, w_mem_v,
                       g_mix_out, w_out, g_ffn_norm, w_ff_gate, w_ff_up, ffn_conv_w, ffn_conv_b,
                       w_ff_down, g_final)

    y_p, pool_p, conv_p, h_p, ffn_p, mkt, mvt = _prompt_layer(x_prompt, mem_prompt, w)

    xs = x_sample.reshape(nb, nt * D_MODEL)
    spool = jnp.transpose(state_pool[0], (1, 0, 2))
    sconv = jnp.transpose(state_rnn_conv[0], (1, 0, 2))
    sffn = state_ffn_conv[0].reshape(nb, (FFN_CONV - 1) * D_FF)
    mix, pool_s, conv_s, h_s = _sample_mixer(xs, spool, sconv, state_rnn_h[0],
                                             _mem_major(cache_mem_k[0]), _mem_major(cache_mem_v[0]), w)
    y_s, ffn_s = _sample_ffn(xs, mix, sffn, w)

    return (y_p,
            y_s.reshape(nb, nt, D_MODEL),
            pool_p[:, POOL_CTX_ROWS - POOL_CTX:][None],
            conv_p[:, SUBLANES - (RNN_CONV - 1):][None],
            h_p.reshape(1, bp, D_RNN),
            ffn_p[:, SUBLANES - (FFN_CONV - 1):][None],
            _mem_minor(mkt)[None],
            _mem_minor(mvt)[None],
            jnp.transpose(pool_s, (1, 0, 2))[None],
            jnp.transpose(conv_s, (1, 0, 2))[None],
            h_s[None],
            ffn_s.reshape(1, nb, FFN_CONV - 1, D_FF))
```

```python
import functools
import math

import jax
import jax.numpy as jnp
from jax import lax
from jax.experimental import pallas as pl
from jax.experimental.pallas import tpu as pltpu

D_MODEL = 1024
PAST_LEN = 16384
D_POOL = D_MODEL // 4
POOL_WINDOWS = (2, 4, 8, 16)
POOL_GW = D_POOL // len(POOL_WINDOWS)
POOL_CTX = max(POOL_WINDOWS) - 1
D_RNN = D_MODEL // 2
RG_BLOCKS = 8
RNN_CONV = 4
RG_C = 8.0
XA_HEADS = 4
XA_HEAD_DIM = 64
D_XA = XA_HEADS * XA_HEAD_DIM
D_IN = D_POOL + 2 * D_RNN + D_XA
N_MEM = 256
D_FF = 3 * D_MODEL
FFN_CONV = 3
EPS = 1e-6

SUBLANES = 8
LANES = 128
MXU_DIM = 256
XA_LANE_HALVES = D_XA // LANES
RG_TILES = D_RNN // MXU_DIM
POOL_CTX_ROWS = 16
TOKEN_TILE = 512
FF_CHUNK = 512
FF_CHUNKS = D_FF // FF_CHUNK
SEQ_CHUNK = 8
VMEM_LIMIT = 60 * 1024 * 1024

BF16 = jnp.bfloat16
F32 = jnp.float32


def _rms(x, g):
    ms = jnp.mean(x * x, axis=-1, keepdims=True)
    return x * lax.rsqrt(ms + EPS) * g


def _group_rms(x):
    ms = jnp.mean(x * x, axis=-1, keepdims=True)
    return x * lax.rsqrt(ms + EPS)


def _gelu_tanh(x):
    c = math.sqrt(2.0 / math.pi)
    half = 0.5 * x
    return half * jnp.tanh(x * (c + (c * 0.044715) * (x * x))) + half


def _softplus(z):
    return jnp.maximum(z, 0.0) + jnp.log1p(jnp.exp(-jnp.abs(z)))


def _dot(a, b):
    return jnp.dot(a.astype(BF16), b, preferred_element_type=F32)


def _gate_dots(xc, wrg_ref, brg_ref):
    out = []
    for gate in range(2):
        parts = [_dot(xc[:, j * MXU_DIM:(j + 1) * MXU_DIM], wrg_ref[gate, j])
                 for j in range(RG_TILES)]
        out.append(jnp.concatenate(parts, axis=1) + brg_ref[gate:gate + 1])
    return out


def _rg_coeffs(xc, r_pre, i_pre, lam):
    r = jax.nn.sigmoid(r_pre)
    i = jax.nn.sigmoid(i_pre)
    log_a = (-RG_C) * r * _softplus(-lam)
    a = jnp.exp(log_a)
    mult = jnp.sqrt(-jnp.tanh(log_a) * (a * a + 1.0))
    return a, mult * i * xc


def _pool_select(p2, p4, p8, p16):
    lane = lax.broadcasted_iota(jnp.int32, p2.shape, 1)
    return jnp.where(lane < POOL_GW, p2,
                     jnp.where(lane < 2 * POOL_GW, p4,
                               jnp.where(lane < 3 * POOL_GW, p8, p16)))


def _pool_window_lane(shape):
    lane = lax.broadcasted_iota(jnp.int32, shape, 1)
    return jnp.where(lane < POOL_GW, POOL_WINDOWS[0],
                     jnp.where(lane < 2 * POOL_GW, POOL_WINDOWS[1],
                               jnp.where(lane < 3 * POOL_GW, POOL_WINDOWS[2], POOL_WINDOWS[3])))


def _head_mask(shape, h):
    lane = lax.broadcasted_iota(jnp.int32, shape, 1)
    return (lane >= h * XA_HEAD_DIM) & (lane < (h + 1) * XA_HEAD_DIM)


def _softmax_rows(s):
    m = jnp.max(s, axis=-1, keepdims=True)
    e = jnp.exp(s - m)
    return e / jnp.sum(e, axis=-1, keepdims=True)


def _shift_rows(ext, k, ctx_rows):
    return pltpu.roll(ext, k, 0)[ctx_rows:]


def _scan_rows(a, b, h_in):
    n, c = a.shape
    groups = n // SUBLANES
    a3 = a.reshape(groups, SUBLANES, c)
    b3 = b.reshape(groups, SUBLANES, c)
    sub = lax.broadcasted_iota(jnp.int32, (1, SUBLANES, c), 1)
    k = 1
    while k < SUBLANES:
        valid = sub >= k
        a_prev = jnp.where(valid, pltpu.roll(a3, k, 1), 1.0)
        b_prev = jnp.where(valid, pltpu.roll(b3, k, 1), 0.0)
        b3 = a3 * b_prev + b3
        a3 = a3 * a_prev
        k *= 2
    carry = h_in
    hs = []
    for g in range(groups):
        h = a3[g] * carry + b3[g]
        hs.append(h)
        carry = h[SUBLANES - 1:SUBLANES]
    return jnp.concatenate(hs, axis=0)


def _ffn_chunk(xn2, c, wfg_ref, wfu_ref, fcw_ref, fcb_ref, ffn_ctx, offn_ref, act_scr):
    tt = xn2.shape[0]
    cols = slice(c * FF_CHUNK, (c + 1) * FF_CHUNK)
    gate = jnp.dot(xn2, wfg_ref[...], preferred_element_type=F32)
    up = jnp.dot(xn2, wfu_ref[...], preferred_element_type=F32)
    ext = jnp.concatenate([ffn_ctx[:, cols], gate], axis=0)
    fw = fcw_ref[:, cols]
    gate_c = fcb_ref[:, cols] + fw[FFN_CONV - 1:FFN_CONV] * gate
    for k in range(1, FFN_CONV):
        gate_c = gate_c + fw[FFN_CONV - 1 - k:FFN_CONV - k] * _shift_rows(ext, k, SUBLANES)
    new_ffn = ext[tt:]
    ffn_ctx[:, cols] = new_ffn
    offn_ref[0, :, cols] = new_ffn
    act_scr[:, cols] = (_gelu_tanh(gate_c) * up).astype(BF16)


def _prompt_kernel(*refs, nt, n_tiles):
    (x_ref, mem_ref, gmem_ref, wkv_ref, g1_ref, win_ref, wpool_ref, pscale_ref, cw_ref, cb_ref,
     wrg_ref, brg_ref, lam_ref, gout_ref, wout_ref, g2_ref) = refs[:16]
    wfg_refs = refs[16:16 + FF_CHUNKS]
    wfu_refs = refs[16 + FF_CHUNKS:16 + 2 * FF_CHUNKS]
    (fcw_ref, fcb_ref, wfd_ref, gfin_ref,
     y_ref, opool_ref, oconv_ref, oh_ref, offn_ref, mk_ref, mv_ref,
     pool_ctx, conv_ctx, h_ctx, ffn_ctx, kt_scr, v_scr, act_scr, x1_scr, xn2_scr
     ) = refs[16 + 2 * FF_CHUNKS:]
    s = pl.program_id(0)
    tm = lax.rem(jnp.minimum(s, n_tiles - 1), nt)
    tf = lax.rem(jnp.maximum(s - 1, 0), nt)
    tt = x_ref.shape[1]

    @pl.when(s == 0)
    def _no_previous_tile():
        x1_scr[...] = jnp.zeros_like(x1_scr)
        xn2_scr[...] = jnp.zeros_like(xn2_scr)

    @pl.when(tm == 0)
    def _start_sequence():
        pool_ctx[...] = jnp.zeros_like(pool_ctx)
        conv_ctx[...] = jnp.zeros_like(conv_ctx)
        h_ctx[...] = jnp.zeros_like(h_ctx)
        mn = _rms(mem_ref[0], gmem_ref[...])
        kv = _dot(mn, wkv_ref[...])
        kt = kv[:, :D_XA].T
        mk_ref[0] = kt
        mv_ref[0] = kv[:, D_XA:].T
        kt_scr[...] = kt.astype(BF16)
        v_scr[...] = kv[:, D_XA:].astype(BF16)

    @pl.when(tf == 0)
    def _start_ffn_sequence():
        ffn_ctx[...] = jnp.zeros_like(ffn_ctx)

    xn2_prev = xn2_scr[...]
    ffn = functools.partial(_ffn_chunk, xn2_prev, fcw_ref=fcw_ref, fcb_ref=fcb_ref,
                            ffn_ctx=ffn_ctx, offn_ref=offn_ref, act_scr=act_scr)

    def attend(heads, o_xa):
        for hd in heads:
            qm = jnp.where(_head_mask(qs.shape, hd), qs, 0.0).astype(BF16)
            p = _softmax_rows(jnp.dot(qm, ktb, preferred_element_type=F32))
            pv = jnp.dot(p.astype(BF16), vb, preferred_element_type=F32)
            o_xa = jnp.where(_head_mask(pv.shape, hd), pv, o_xa)
        return o_xa

    ffn(c=0, wfg_ref=wfg_refs[0], wfu_ref=wfu_refs[0])
    x = x_ref[0]
    proj = _dot(_rms(x, g1_ref[...]), win_ref[...])
    u_pool = proj[:, :D_POOL]
    x_rnn = proj[:, D_POOL:D_POOL + D_RNN]
    g_rnn = proj[:, D_POOL + D_RNN:D_POOL + 2 * D_RNN]
    q = proj[:, D_POOL + 2 * D_RNN:]

    ext = jnp.concatenate([pool_ctx[...], u_pool], axis=0)
    p2 = ext + pltpu.roll(ext, 1, 0)
    p4 = p2 + pltpu.roll(p2, 2, 0)
    p8 = p4 + pltpu.roll(p4, 4, 0)
    p16 = p8 + pltpu.roll(p8, 8, 0)
    sm = _pool_select(p2, p4, p8, p16)[POOL_CTX_ROWS:]
    pos = tm * tt + lax.broadcasted_iota(jnp.int32, sm.shape, 0)
    cnt = jnp.minimum(pos + 1, _pool_window_lane(sm.shape)).astype(F32)
    diff = sm / cnt - u_pool
    o_pool = _dot(diff, wpool_ref[...]) * pscale_ref[...]
    new_pool = ext[tt:]
    pool_ctx[...] = new_pool

    ext = jnp.concatenate([conv_ctx[...], x_rnn], axis=0)
    cw = cw_ref[...]
    xc = cb_ref[...] + cw[RNN_CONV - 1:RNN_CONV] * x_rnn
    for k in range(1, RNN_CONV):
        xc = xc + cw[RNN_CONV - 1 - k:RNN_CONV - k] * _shift_rows(ext, k, SUBLANES)
    new_conv = ext[tt:]
    conv_ctx[...] = new_conv

    ffn(c=1, wfg_ref=wfg_refs[1], wfu_ref=wfu_refs[1])
    r_pre, i_pre = _gate_dots(xc, wrg_ref, brg_ref)
    a, bt = _rg_coeffs(xc, r_pre, i_pre, lam_ref[...])
    ffn(c=2, wfg_ref=wfg_refs[2], wfu_ref=wfu_refs[2])
    h = _scan_rows(a, bt, h_ctx[...])
    h_last = h[tt - 1:tt]
    h_ctx[...] = h_last
    o_rnn = _gelu_tanh(g_rnn) * h

    ktb = kt_scr[...]
    vb = v_scr[...]
    qs = q * (1.0 / math.sqrt(XA_HEAD_DIM))
    ffn(c=3, wfg_ref=wfg_refs[3], wfu_ref=wfu_refs[3])
    o_xa = attend((0,), jnp.zeros((tt, D_XA), F32))
    ffn(c=4, wfg_ref=wfg_refs[4], wfu_ref=wfu_refs[4])
    o_xa = attend((1,), o_xa)
    ffn(c=5, wfg_ref=wfg_refs[5], wfu_ref=wfu_refs[5])

    x2 = x1_scr[...] + jnp.dot(act_scr[...], wfd_ref[...], preferred_element_type=F32)
    o_xa = attend((2, 3), o_xa)
    mix = jnp.concatenate([_group_rms(o_pool), _group_rms(o_rnn), _group_rms(o_xa)], axis=-1)

    x1 = x + _dot(mix * gout_ref[...], wout_ref[...])
    y_ref[0] = _rms(x2, gfin_ref[...])
    xn2 = _rms(x1, g2_ref[...]).astype(BF16)

    x1_scr[...] = x1
    xn2_scr[...] = xn2

    @pl.when(s < n_tiles)
    def _emit_mixer_state():
        opool_ref[0] = new_pool
        oconv_ref[0] = new_conv
        oh_ref[0] = h_last


def _const_spec(shape):
    zeros = (0,) * len(shape)
    return pl.BlockSpec(shape, lambda *_: zeros, pipeline_mode=pl.Buffered(1))


def _prompt_layer(x, mem, w):
    batch, seq, _ = x.shape
    nt = seq // TOKEN_TILE
    n_tiles = batch * nt
    consts = [w['g_mem'], w['w_kv'], w['g_mix_norm'], w['w_in'], w['w_pool'], w['pool_scale'],
              w['rnn_conv_w'], w['rnn_conv_b'], w['w_rg'], w['b_rg'], w['rg_lambda'],
              w['g_mix_out'], w['w_out'], w['g_ffn_norm'], *w['w_ff_gate'], *w['w_ff_up'],
              w['ffn_conv_w'], w['ffn_conv_b'], w['w_ff_down'], w['g_final']]
    mixer_tile = lambda s: jnp.minimum(s, n_tiles - 1)
    ffn_tile = lambda s: jnp.maximum(s - 1, 0)
    mixer_seq = lambda s: (mixer_tile(s) // nt, 0, 0)
    ffn_seq = lambda s: (ffn_tile(s) // nt, 0, 0)
    in_specs = [pl.BlockSpec((1, TOKEN_TILE, D_MODEL),
                             lambda s: (mixer_tile(s) // nt, mixer_tile(s) % nt, 0)),
                pl.BlockSpec((1, N_MEM, D_MODEL), mixer_seq, pipeline_mode=pl.Buffered(1))]
    in_specs += [_const_spec(c.shape) for c in consts]
    out_shape = (jax.ShapeDtypeStruct((batch, seq, D_MODEL), F32),
                 jax.ShapeDtypeStruct((batch, POOL_CTX_ROWS, D_POOL), F32),
                 jax.ShapeDtypeStruct((batch, SUBLANES, D_RNN), F32),
                 jax.ShapeDtypeStruct((batch, 1, D_RNN), F32),
                 jax.ShapeDtypeStruct((batch, SUBLANES, D_FF), F32),
                 jax.ShapeDtypeStruct((batch, D_XA, N_MEM), F32),
                 jax.ShapeDtypeStruct((batch, D_XA, N_MEM), F32))
    out_specs = (pl.BlockSpec((1, TOKEN_TILE, D_MODEL),
                              lambda s: (ffn_tile(s) // nt, ffn_tile(s) % nt, 0)),
                 pl.BlockSpec((1, POOL_CTX_ROWS, D_POOL), mixer_seq),
                 pl.BlockSpec((1, SUBLANES, D_RNN), mixer_seq),
                 pl.BlockSpec((1, 1, D_RNN), mixer_seq),
                 pl.BlockSpec((1, SUBLANES, D_FF), ffn_seq),
                 pl.BlockSpec((1, D_XA, N_MEM), mixer_seq),
                 pl.BlockSpec((1, D_XA, N_MEM), mixer_seq))
    scratch = [pltpu.VMEM((POOL_CTX_ROWS, D_POOL), F32),
               pltpu.VMEM((SUBLANES, D_RNN), F32),
               pltpu.VMEM((1, D_RNN), F32),
               pltpu.VMEM((SUBLANES, D_FF), F32),
               pltpu.VMEM((D_XA, N_MEM), BF16),
               pltpu.VMEM((N_MEM, D_XA), BF16),
               pltpu.VMEM((TOKEN_TILE, D_FF), BF16),
               pltpu.VMEM((TOKEN_TILE, D_MODEL), F32),
               pltpu.VMEM((TOKEN_TILE, D_MODEL), BF16)]
    return pl.pallas_call(
        functools.partial(_prompt_kernel, nt=nt, n_tiles=n_tiles),
        grid=(n_tiles + 1,),
        in_specs=in_specs,
        out_specs=out_specs,
        out_shape=out_shape,
        scratch_shapes=scratch,
        compiler_params=pltpu.CompilerParams(
            dimension_semantics=("arbitrary",), vmem_limit_bytes=VMEM_LIMIT),
        name="prompt_layer",
    )(x, mem, *consts)


def _sample_mixer_kernel(xs_ref, spool_ref, sconv_ref, sh_ref, ckt_ref, cvt_ref,
                         g1_ref, win_ref, wpool_ref, pscale_ref, cw_ref, cb_ref,
                         wrg_ref, brg_ref, lam_ref,
                         mix_ref, npool_ref, nconv_ref, nh_ref,
                         q_scr, o_scr):
    i = pl.program_id(0)
    nb, nt, _ = xs_ref.shape

    @pl.when(i == 0)
    def _dense():
        x = jnp.concatenate([xs_ref[:, t, :] for t in range(nt)], axis=0)
        proj = _dot(_rms(x, g1_ref[...]), win_ref[...])
        u_pool = proj[:, :D_POOL]
        x_rnn = proj[:, D_POOL:D_POOL + D_RNN]
        g_rnn = proj[:, D_POOL + D_RNN:D_POOL + 2 * D_RNN]
        q = proj[:, D_POOL + 2 * D_RNN:] * (1.0 / math.sqrt(XA_HEAD_DIM))
        rows = lambda v, t: v[t * nb:(t + 1) * nb]

        full = [spool_ref[j] for j in range(POOL_CTX)] + [rows(u_pool, t) for t in range(nt)]
        n = len(full)
        p2 = {j: full[j] + full[j - 1] for j in range(1, n)}
        p4 = {j: p2[j] + p2[j - 2] for j in range(3, n)}
        p8 = {j: p4[j] + p4[j - 4] for j in range(7, n)}
        p16 = {j: p8[j] + p8[j - 8] for j in range(15, n)}
        win = _pool_window_lane((nb, D_POOL))
        diffs = []
        for t in range(nt):
            j = POOL_CTX + t
            cnt = jnp.minimum(PAST_LEN + t + 1, win).astype(F32)
            diffs.append(_pool_select(p2[j], p4[j], p8[j], p16[j]) / cnt - full[j])
        o_pool = _dot(jnp.concatenate(diffs, axis=0), wpool_ref[...]) * pscale_ref[...]
        for j in range(POOL_CTX):
            npool_ref[j] = full[nt + j]

        fullc = [sconv_ref[j] for j in range(RNN_CONV - 1)] + [rows(x_rnn, t) for t in range(nt)]
        cw = cw_ref[...]
        xcs = []
        for t in range(nt):
            acc = cb_ref[...] + cw[0:1] * fullc[t]
            for k in range(1, RNN_CONV):
                acc = acc + cw[k:k + 1] * fullc[t + k]
            xcs.append(acc)
        for j in range(RNN_CONV - 1):
            nconv_ref[j] = fullc[nt + j]
        xc = jnp.concatenate(xcs, axis=0)
        r_pre, i_pre = _gate_dots(xc, wrg_ref, brg_ref)
        a, bt = _rg_coeffs(xc, r_pre, i_pre, lam_ref[...])
        h = sh_ref[...]
        hs = []
        for t in range(nt):
            h = rows(a, t) * h + rows(bt, t)
            hs.append(h)
        nh_ref[...] = h
        o_rnn = _gelu_tanh(g_rnn) * jnp.concatenate(hs, axis=0)

        mix_ref[:, :D_POOL] = _group_rms(o_pool)
        mix_ref[:, D_POOL:D_POOL + D_RNN] = _group_rms(o_rnn)
        q_scr[...] = jnp.zeros_like(q_scr)
        for t in range(nt):
            for half in range(XA_LANE_HALVES):
                q_scr[half, pl.ds(t, nb, stride=SUBLANES), :] = (
                    rows(q, t)[:, half * LANES:(half + 1) * LANES])

    for j in range(SEQ_CHUNK):
        seq_rows = pl.ds(pl.multiple_of((i * SEQ_CHUNK + j) * SUBLANES, SUBLANES), SUBLANES)
        qb = jnp.concatenate([q_scr[half, seq_rows, :] for half in range(XA_LANE_HALVES)],
                             axis=1)
        qbd = jnp.concatenate(
            [jnp.where(_head_mask(qb.shape, hd), qb, 0.0) for hd in range(XA_HEADS)], axis=0)
        ktb = ckt_ref[j].astype(BF16)
        vtb = cvt_ref[j].astype(BF16)
        p = _softmax_rows(jnp.dot(qbd.astype(BF16), ktb, preferred_element_type=F32))
        pv = lax.dot_general(p.astype(BF16), vtb, (((1,), (1,)), ((), ())),
                             preferred_element_type=F32)
        ob = jnp.zeros((SUBLANES, D_XA), F32)
        for hd in range(XA_HEADS):
            part = pv[hd * SUBLANES:(hd + 1) * SUBLANES]
            ob = jnp.where(_head_mask(part.shape, hd), part, ob)
        for half in range(XA_LANE_HALVES):
            o_scr[half, seq_rows, :] = ob[:, half * LANES:(half + 1) * LANES]

    @pl.when(i == pl.num_programs(0) - 1)
    def _finish():
        for t in range(nt):
            o_t = jnp.concatenate([o_scr[half, pl.ds(t, nb, stride=SUBLANES), :]
                                   for half in range(XA_LANE_HALVES)], axis=1)
            mix_ref[t * nb:(t + 1) * nb, D_POOL + D_RNN:] = _group_rms(o_t)


def _sample_mixer(xs, spool, sconv, sh, ckt, cvt, w):
    nb, nt, _ = xs.shape
    consts_in = [xs, spool, sconv, sh]
    consts_w = [w['g_mix_norm'], w['w_in'], w['w_pool'], w['pool_scale'], w['rnn_conv_w'],
                w['rnn_conv_b'], w['w_rg'], w['b_rg'], w['rg_lambda']]
    kv_spec = pl.BlockSpec((SEQ_CHUNK, D_XA, N_MEM), lambda i: (i, 0, 0))
    in_specs = ([_const_spec(c.shape) for c in consts_in] + [kv_spec, kv_spec]
                + [_const_spec(c.shape) for c in consts_w])
    out_shape = (jax.ShapeDtypeStruct((nt * nb, D_MODEL), F32),
                 jax.ShapeDtypeStruct(spool.shape, F32),
                 jax.ShapeDtypeStruct(sconv.shape, F32),
                 jax.ShapeDtypeStruct(sh.shape, F32))
    out_specs = tuple(pl.BlockSpec(s.shape, lambda i, n=len(s.shape): (0,) * n) for s in out_shape)
    return pl.pallas_call(
        _sample_mixer_kernel,
        grid=(nb // SEQ_CHUNK,),
        in_specs=in_specs,
        out_specs=out_specs,
        out_shape=out_shape,
        scratch_shapes=[pltpu.VMEM((XA_LANE_HALVES, nb * SUBLANES, LANES), F32),
                        pltpu.VMEM((XA_LANE_HALVES, nb * SUBLANES, LANES), F32)],
        compiler_params=pltpu.CompilerParams(
            dimension_semantics=("arbitrary",), vmem_limit_bytes=VMEM_LIMIT),
        name="sample_mixer",
    )(*consts_in, ckt, cvt, *consts_w)


def _sample_ffn_kernel(*refs):
    xs_ref, mix_ref, sffn_ref, gout_ref, wout_ref, g2_ref = refs[:6]
    wfg_refs = refs[6:6 + FF_CHUNKS]
    wfu_refs = refs[6 + FF_CHUNKS:6 + 2 * FF_CHUNKS]
    fcw_ref, fcb_ref, wfd_ref, gfin_ref, y_ref, nffn_ref, act_scr = refs[6 + 2 * FF_CHUNKS:]
    nb, nt, _ = xs_ref.shape
    x = jnp.concatenate([xs_ref[:, t, :] for t in range(nt)], axis=0)
    x1 = x + _dot(mix_ref[...] * gout_ref[...], wout_ref[...])
    xn2 = _rms(x1, g2_ref[...]).astype(BF16)
    for c in range(FF_CHUNKS):
        cols = slice(c * FF_CHUNK, (c + 1) * FF_CHUNK)
        gate = jnp.dot(xn2, wfg_refs[c][...], preferred_element_type=F32)
        up = jnp.dot(xn2, wfu_refs[c][...], preferred_element_type=F32)
        full = [sffn_ref[:, j, cols] for j in range(FFN_CONV - 1)]
        full += [gate[t * nb:(t + 1) * nb] for t in range(nt)]
        fw = fcw_ref[:, cols]
        convs = []
        for t in range(nt):
            a = fcb_ref[:, cols] + fw[0:1] * full[t]
            for k in range(1, FFN_CONV):
                a = a + fw[k:k + 1] * full[t + k]
            convs.append(a)
        for j in range(FFN_CONV - 1):
            nffn_ref[:, j, cols] = full[nt + j]
        gate_c = jnp.concatenate(convs, axis=0)
        act_scr[:, cols] = (_gelu_tanh(gate_c) * up).astype(BF16)
    x2 = x1 + jnp.dot(act_scr[...], wfd_ref[...], preferred_element_type=F32)
    y = _rms(x2, gfin_ref[...])
    for t in range(nt):
        y_ref[:, t, :] = y[t * nb:(t + 1) * nb]


def _sample_ffn(xs, mix, sffn, w):
    ins = [xs, mix, sffn, w['g_mix_out'], w['w_out'], w['g_ffn_norm'], *w['w_ff_gate'],
           *w['w_ff_up'], w['ffn_conv_w'], w['ffn_conv_b'], w['w_ff_down'], w['g_final']]
    out_shape = (jax.ShapeDtypeStruct(xs.shape, F32), jax.ShapeDtypeStruct(sffn.shape, F32))
    return pl.pallas_call(
        _sample_ffn_kernel,
        grid=(1,),
        in_specs=[_const_spec(c.shape) for c in ins],
        out_specs=tuple(pl.BlockSpec(s.shape, lambda i: (0, 0, 0)) for s in out_shape),
        out_shape=out_shape,
        scratch_shapes=[pltpu.VMEM((mix.shape[0], D_FF), BF16)],
        compiler_params=pltpu.CompilerParams(
            dimension_semantics=("arbitrary",), vmem_limit_bytes=VMEM_LIMIT),
        name="sample_ffn",
    )(*ins)


def _block_diag(blocks):
    n, c, d = blocks.shape
    eye = jnp.eye(n, dtype=blocks.dtype)
    return (eye[:, None, :, None] * blocks[:, :, None, :]).reshape(n * c, n * d)


def _diag_tiles(blocks):
    bd = _block_diag(blocks)
    return jnp.stack([bd[j * MXU_DIM:(j + 1) * MXU_DIM, j * MXU_DIM:(j + 1) * MXU_DIM]
                      for j in range(bd.shape[0] // MXU_DIM)])


def _column_chunks(wmat):
    return [wmat[:, c * FF_CHUNK:(c + 1) * FF_CHUNK].astype(BF16)
            for c in range(wmat.shape[1] // FF_CHUNK)]


def _layer_weights(l, g_mix_norm, w_in, w_pool, pool_scale, rnn_conv_w, rnn_conv_b, w_rg_a,
                   b_rg_a, w_rg_x, b_rg_x, rg_lambda, g_mem_norm, w_mem_k, w_mem_v, g_mix_out,
                   w_out, g_ffn_norm, w_ff_gate, w_ff_up, ffn_conv_w, ffn_conv_b, w_ff_down,
                   g_final):
    row = lambda v: v.reshape(1, -1)
    return {
        'g_mix_norm': row(g_mix_norm[l]),
        'w_in': w_in[l].astype(BF16),
        'w_pool': _block_diag(w_pool[l]).astype(BF16),
        'pool_scale': row(pool_scale[l]),
        'rnn_conv_w': rnn_conv_w[l],
        'rnn_conv_b': row(rnn_conv_b[l]),
        'w_rg': jnp.stack([_diag_tiles(w_rg_a[l]), _diag_tiles(w_rg_x[l])]).astype(BF16),
        'b_rg': jnp.stack([b_rg_a[l], b_rg_x[l]]),
        'rg_lambda': row(rg_lambda[l]),
        'g_mem': row(g_mem_norm[l]),
        'w_kv': jnp.concatenate([w_mem_k[l], w_mem_v[l]], axis=1).astype(BF16),
        'g_mix_out': row(g_mix_out[l]),
        'w_out': w_out[l].astype(BF16),
        'g_ffn_norm': row(g_ffn_norm[l]),
        'w_ff_gate': _column_chunks(w_ff_gate[l]),
        'w_ff_up': _column_chunks(w_ff_up[l]),
        'ffn_conv_w': ffn_conv_w[l],
        'ffn_conv_b': row(ffn_conv_b[l]),
        'w_ff_down': w_ff_down[l].astype(BF16),
        'g_final': row(g_final),
    }


def _mem_major(cache):
    b = cache.shape[0]
    return jnp.transpose(cache, (0, 2, 3, 1)).reshape(b, D_XA, N_MEM)


def _mem_minor(kt):
    b = kt.shape[0]
    return jnp.transpose(kt.reshape(b, XA_HEADS, XA_HEAD_DIM, N_MEM), (0, 3, 1, 2))


def kernel(x_prompt, x_sample, mem_prompt, state_pool, state_rnn_conv, state_rnn_h, state_ffn_conv, cache_mem_k, cache_mem_v, g_mix_norm, w_in, w_pool, pool_scale, rnn_conv_w, rnn_conv_b, w_rg_a, b_rg_a, w_rg_x, b_rg_x, rg_lambda, g_mem_norm, w_mem_k, w_mem_v, g_mix_out, w_out, g_ffn_norm, w_ff_gate, w_ff_up, ffn_conv_w, ffn_conv_b, w_ff_down, g_final):
    depth = w_in.shape[0]
    assert depth == 1, "the final norm is fused into the single layer"
    bp = x_prompt.shape[0]
    w = _layer_weights(0, g_mix_norm, w_in, w_pool, pool_scale, rnn_conv_w, rnn_conv_b, w_rg_a,
                       b_rg_a, w_rg_x, b_rg_x, rg_lambda, g_mem_norm, w_mem_k, w_mem_v,
                       g_mix_out, w_out, g_ffn_norm, w_ff_gate, w_ff_up, ffn_conv_w, ffn_conv_b,
                       w_ff_down, g_final)

    y_p, pool_p, conv_p, h_p, ffn_p, mkt, mvt = _prompt_layer(x_prompt, mem_prompt, w)

    spool = jnp.transpose(state_pool[0], (1, 0, 2))
    sconv = jnp.transpose(state_rnn_conv[0], (1, 0, 2))
    mix, pool_s, conv_s, h_s = _sample_mixer(x_sample, spool, sconv, state_rnn_h[0],
                                             _mem_major(cache_mem_k[0]), _mem_major(cache_mem_v[0]), w)
    y_s, ffn_s = _sample_ffn(x_sample, mix, state_ffn_conv[0], w)

    return (y_p,
            y_s,
            pool_p[:, POOL_CTX_ROWS - POOL_CTX:][None],
            conv_p[:, SUBLANES - (RNN_CONV - 1):][None],
            h_p.reshape(1, bp, D_RNN),
            ffn_p[:, SUBLANES - (FFN_CONV - 1):][None],
            _mem_minor(mkt)[None],
            _mem_minor(mvt)[None],
            jnp.transpose(pool_s, (1, 0, 2))[None],
            jnp.transpose(conv_s, (1, 0, 2))[None],
            h_s[None],
            ffn_s[None])
```

```python
import functools
import math

import jax
import jax.numpy as jnp
from jax import lax
from jax.experimental import pallas as pl
from jax.experimental.pallas import tpu as pltpu

D_MODEL = 1024
PAST_LEN = 16384
D_POOL = D_MODEL // 4
POOL_WINDOWS = (2, 4, 8, 16)
POOL_GW = D_POOL // len(POOL_WINDOWS)
POOL_CTX = max(POOL_WINDOWS) - 1
D_RNN = D_MODEL // 2
RG_BLOCKS = 8
RNN_CONV = 4
RG_C = 8.0
XA_HEADS = 4
XA_HEAD_DIM = 64
D_XA = XA_HEADS * XA_HEAD_DIM
D_IN = D_POOL + 2 * D_RNN + D_XA
N_MEM = 256
D_FF = 3 * D_MODEL
FFN_CONV = 3
EPS = 1e-6

SUBLANES = 8
LANES = 128
MXU_DIM = 256
XA_LANE_HALVES = D_XA // LANES
RG_TILES = D_RNN // MXU_DIM
POOL_CTX_ROWS = 16
TOKEN_TILE = 512
FF_CHUNK = 512
FF_CHUNKS = D_FF // FF_CHUNK
SEQ_CHUNK = 8
VMEM_LIMIT = 60 * 1024 * 1024

BF16 = jnp.bfloat16
F32 = jnp.float32


def _rms(x, g):
    ms = jnp.mean(x * x, axis=-1, keepdims=True)
    return x * lax.rsqrt(ms + EPS) * g


def _group_rms(x):
    ms = jnp.mean(x * x, axis=-1, keepdims=True)
    return x * lax.rsqrt(ms + EPS)


def _gelu_tanh(x):
    c = math.sqrt(2.0 / math.pi)
    half = 0.5 * x
    return half * jnp.tanh(x * (c + (c * 0.044715) * (x * x))) + half


def _gelu_tanh_x2(x):
    c = math.sqrt(2.0 / math.pi)
    return x * jnp.tanh(x * (c + (c * 0.044715) * (x * x))) + x


def _softplus(z):
    return jnp.maximum(z, 0.0) + jnp.log1p(jnp.exp(-jnp.abs(z)))


def _dot(a, b):
    return jnp.dot(a.astype(BF16), b, preferred_element_type=F32)


def _gate_dots(xc, wrg_ref, brg_ref):
    out = []
    for gate in range(2):
        parts = [_dot(xc[:, j * MXU_DIM:(j + 1) * MXU_DIM], wrg_ref[gate, j])
                 for j in range(RG_TILES)]
        out.append(jnp.concatenate(parts, axis=1) + brg_ref[gate:gate + 1])
    return out


def _rg_coeffs(xc, r_half, i_half, lam):
    half_c = (-0.5 * RG_C) * _softplus(-lam)
    log_a = half_c * jnp.tanh(r_half) + half_c
    a = jnp.exp(log_a)
    y = jnp.tanh(log_a) * (-1.0 - a * a)
    mult = jnp.where(y > 0.0, y * lax.rsqrt(y), 0.0)
    i = 0.5 * jnp.tanh(i_half) + 0.5
    return a, mult * i * xc


def _pool_select(p2, p4, p8, p16):
    lane = lax.broadcasted_iota(jnp.int32, p2.shape, 1)
    return jnp.where(lane < POOL_GW, p2,
                     jnp.where(lane < 2 * POOL_GW, p4,
                               jnp.where(lane < 3 * POOL_GW, p8, p16)))


def _pool_window_lane(shape):
    lane = lax.broadcasted_iota(jnp.int32, shape, 1)
    return jnp.where(lane < POOL_GW, POOL_WINDOWS[0],
                     jnp.where(lane < 2 * POOL_GW, POOL_WINDOWS[1],
                               jnp.where(lane < 3 * POOL_GW, POOL_WINDOWS[2], POOL_WINDOWS[3])))


def _head_mask(shape, h):
    lane = lax.broadcasted_iota(jnp.int32, shape, 1)
    return (lane >= h * XA_HEAD_DIM) & (lane < (h + 1) * XA_HEAD_DIM)


def _softmax_rows(s):
    m = jnp.max(s, axis=-1, keepdims=True)
    e = jnp.exp(s - m)
    return e / jnp.sum(e, axis=-1, keepdims=True)


def _shift_rows(ext, k, ctx_rows):
    return pltpu.roll(ext, k, 0)[ctx_rows:]


def _scan_rows(a, b, h_in):
    n, c = a.shape
    groups = n // SUBLANES
    a3 = a.reshape(groups, SUBLANES, c)
    b3 = b.reshape(groups, SUBLANES, c)
    sub = lax.broadcasted_iota(jnp.int32, (1, SUBLANES, c), 1)
    k = 1
    while k < SUBLANES:
        valid = sub >= k
        a_prev = jnp.where(valid, pltpu.roll(a3, k, 1), 1.0)
        b_prev = jnp.where(valid, pltpu.roll(b3, k, 1), 0.0)
        b3 = a3 * b_prev + b3
        a3 = a3 * a_prev
        k *= 2
    carry = h_in
    hs = []
    for g in range(groups):
        h = a3[g] * carry + b3[g]
        hs.append(h)
        carry = h[SUBLANES - 1:SUBLANES]
    return jnp.concatenate(hs, axis=0)


def _ffn_chunk(xn2, c, wfg_ref, wfu_ref, fcw_ref, fcb_ref, ffn_ctx, offn_ref):
    tt = xn2.shape[0]
    cols = slice(c * FF_CHUNK, (c + 1) * FF_CHUNK)
    gate = jnp.dot(xn2, wfg_ref[...], preferred_element_type=F32)
    up_half = jnp.dot(xn2, wfu_ref[...], preferred_element_type=F32)
    ext = jnp.concatenate([ffn_ctx[:, cols], gate], axis=0)
    fw = fcw_ref[:, cols]
    gate_c = fcb_ref[:, cols] + fw[FFN_CONV - 1:FFN_CONV] * gate
    for k in range(1, FFN_CONV):
        gate_c = gate_c + fw[FFN_CONV - 1 - k:FFN_CONV - k] * _shift_rows(ext, k, SUBLANES)
    new_ffn = ext[tt:]
    ffn_ctx[:, cols] = new_ffn
    offn_ref[0, :, cols] = new_ffn
    return (_gelu_tanh_x2(gate_c) * up_half).astype(BF16)


def _prompt_kernel(*refs, nt, n_tiles):
    (x_ref, mem_ref, gmem_ref, wkv_ref, g1_ref, win_ref, wpool_ref, pscale_ref, cw_ref, cb_ref,
     wrg_ref, brg_ref, lam_ref, gout_ref, wout_ref, g2_ref, wfg_ref, wfu_ref,
     fcw_ref, fcb_ref, wfd_ref, gfin_ref,
     y_ref, opool_ref, oconv_ref, oh_ref, offn_ref, mk_ref, mv_ref,
     pool_ctx, conv_ctx, h_ctx, ffn_ctx, kt_scr, v_scr, x1_scr, xn2_scr) = refs
    wfg_refs = [wfg_ref.at[c] for c in range(FF_CHUNKS)]
    wfu_refs = [wfu_ref.at[c] for c in range(FF_CHUNKS)]
    s = pl.program_id(0)
    tm = lax.rem(jnp.minimum(s, n_tiles - 1), nt)
    tf = lax.rem(jnp.maximum(s - 1, 0), nt)
    tt = x_ref.shape[1]

    @pl.when(s == 0)
    def _no_previous_tile():
        x1_scr[...] = jnp.zeros_like(x1_scr)
        xn2_scr[...] = jnp.zeros_like(xn2_scr)

    @pl.when(tm == 0)
    def _start_sequence():
        pool_ctx[...] = jnp.zeros_like(pool_ctx)
        conv_ctx[...] = jnp.zeros_like(conv_ctx)
        h_ctx[...] = jnp.zeros_like(h_ctx)
        mn = _rms(mem_ref[0], gmem_ref[...])
        kv = _dot(mn, wkv_ref[...])
        kt = kv[:, :D_XA].T
        mk_ref[0] = kt
        mv_ref[0] = kv[:, D_XA:].T
        v = kv[:, D_XA:]
        chan = lax.broadcasted_iota(jnp.int32, kt.shape, 0)
        for hd in range(XA_HEADS):
            own = (chan >= hd * XA_HEAD_DIM) & (chan < (hd + 1) * XA_HEAD_DIM)
            kt_scr[:, hd * N_MEM:(hd + 1) * N_MEM] = jnp.where(own, kt, 0.0).astype(BF16)
            v_scr[hd * N_MEM:(hd + 1) * N_MEM, :] = (
                jnp.where(_head_mask(v.shape, hd), v, 0.0).astype(BF16))

    @pl.when(tf == 0)
    def _start_ffn_sequence():
        ffn_ctx[...] = jnp.zeros_like(ffn_ctx)

    xn2_prev = xn2_scr[...]
    acts = []

    def ffn(c):
        acts.append(_ffn_chunk(xn2_prev, c, wfg_refs[c], wfu_refs[c], fcw_ref, fcb_ref,
                               ffn_ctx, offn_ref))

    def head_probs(hd):
        return _softmax_rows(scores[:, hd * N_MEM:(hd + 1) * N_MEM]).astype(BF16)

    ffn(0)
    x = x_ref[0]
    proj = _dot(_rms(x, g1_ref[...]), win_ref[...])
    u_pool = proj[:, :D_POOL]
    x_rnn = proj[:, D_POOL:D_POOL + D_RNN]
    g_rnn = proj[:, D_POOL + D_RNN:D_POOL + 2 * D_RNN]
    q = proj[:, D_POOL + 2 * D_RNN:]

    ext = jnp.concatenate([pool_ctx[...], u_pool], axis=0)
    p2 = ext + pltpu.roll(ext, 1, 0)
    p4 = p2 + pltpu.roll(p2, 2, 0)
    p8 = p4 + pltpu.roll(p4, 4, 0)
    p16 = p8 + pltpu.roll(p8, 8, 0)
    sm = _pool_select(p2, p4, p8, p16)[POOL_CTX_ROWS:]
    head = (POOL_CTX_ROWS, D_POOL)
    win = _pool_window_lane(head)
    pos = tm * tt + lax.broadcasted_iota(jnp.int32, head, 0)
    inv_head = 1.0 / jnp.minimum(pos + 1, win).astype(F32)
    inv_rest = 1.0 / win[:1].astype(F32)
    diff = jnp.concatenate([sm[:POOL_CTX_ROWS] * inv_head, sm[POOL_CTX_ROWS:] * inv_rest],
                           axis=0) - u_pool
    o_pool = _dot(diff, wpool_ref[...]) * pscale_ref[...]
    new_pool = ext[tt:]
    pool_ctx[...] = new_pool

    ext = jnp.concatenate([conv_ctx[...], x_rnn], axis=0)
    cw = cw_ref[...]
    xc = cb_ref[...] + cw[RNN_CONV - 1:RNN_CONV] * x_rnn
    for k in range(1, RNN_CONV):
        xc = xc + cw[RNN_CONV - 1 - k:RNN_CONV - k] * _shift_rows(ext, k, SUBLANES)
    new_conv = ext[tt:]
    conv_ctx[...] = new_conv

    ffn(1)
    r_half, i_half = _gate_dots(xc, wrg_ref, brg_ref)
    a, bt = _rg_coeffs(xc, r_half, i_half, lam_ref[...])
    scores = _dot(q * (1.0 / math.sqrt(XA_HEAD_DIM)), kt_scr[...])

    probs = []
    for hd in range(XA_HEADS):
        ffn(2 + hd)
        probs.append(head_probs(hd))
    o_xa = jnp.dot(jnp.concatenate(probs, axis=1), v_scr[...], preferred_element_type=F32)

    x2 = x1_scr[...] + jnp.dot(jnp.concatenate(acts, axis=1), wfd_ref[...],
                               preferred_element_type=F32)
    h = _scan_rows(a, bt, h_ctx[...])
    h_last = h[tt - 1:tt]
    h_ctx[...] = h_last
    o_rnn = _gelu_tanh(g_rnn) * h
    mix = jnp.concatenate([_group_rms(o_pool), _group_rms(o_rnn), _group_rms(o_xa)], axis=-1)

    x1 = x + _dot(mix * gout_ref[...], wout_ref[...])
    y_ref[0] = _rms(x2, gfin_ref[...])
    xn2 = _rms(x1, g2_ref[...]).astype(BF16)

    x1_scr[...] = x1
    xn2_scr[...] = xn2

    @pl.when(s < n_tiles)
    def _emit_mixer_state():
        opool_ref[0] = new_pool
        oconv_ref[0] = new_conv
        oh_ref[0] = h_last


def _const_spec(shape):
    zeros = (0,) * len(shape)
    return pl.BlockSpec(shape, lambda *_: zeros, pipeline_mode=pl.Buffered(1))


def _prompt_layer(x, mem, w):
    batch, seq, _ = x.shape
    nt = seq // TOKEN_TILE
    n_tiles = batch * nt
    consts = [w['g_mem'], w['w_kv'], w['g_mix_norm'], w['w_in'], w['w_pool'], w['pool_scale'],
              w['rnn_conv_w'], w['rnn_conv_b'], w['w_rg'], w['b_rg'], w['rg_lambda'],
              w['g_mix_out'], w['w_out'], w['g_ffn_norm'], w['w_ff_gate'], w['w_ff_up_half'],
              w['ffn_conv_w'], w['ffn_conv_b'], w['w_ff_down'], w['g_final']]
    mixer_tile = lambda s: jnp.minimum(s, n_tiles - 1)
    ffn_tile = lambda s: jnp.maximum(s - 1, 0)
    mixer_seq = lambda s: (mixer_tile(s) // nt, 0, 0)
    ffn_seq = lambda s: (ffn_tile(s) // nt, 0, 0)
    in_specs = [pl.BlockSpec((1, TOKEN_TILE, D_MODEL),
                             lambda s: (mixer_tile(s) // nt, mixer_tile(s) % nt, 0)),
                pl.BlockSpec((1, N_MEM, D_MODEL), mixer_seq, pipeline_mode=pl.Buffered(1))]
    in_specs += [_const_spec(c.shape) for c in consts]
    out_shape = (jax.ShapeDtypeStruct((batch, seq, D_MODEL), F32),
                 jax.ShapeDtypeStruct((batch, POOL_CTX_ROWS, D_POOL), F32),
                 jax.ShapeDtypeStruct((batch, SUBLANES, D_RNN), F32),
                 jax.ShapeDtypeStruct((batch, 1, D_RNN), F32),
                 jax.ShapeDtypeStruct((batch, SUBLANES, D_FF), F32),
                 jax.ShapeDtypeStruct((batch, D_XA, N_MEM), F32),
                 jax.ShapeDtypeStruct((batch, D_XA, N_MEM), F32))
    out_specs = (pl.BlockSpec((1, TOKEN_TILE, D_MODEL),
                              lambda s: (ffn_tile(s) // nt, ffn_tile(s) % nt, 0)),
                 pl.BlockSpec((1, POOL_CTX_ROWS, D_POOL), mixer_seq),
                 pl.BlockSpec((1, SUBLANES, D_RNN), mixer_seq),
                 pl.BlockSpec((1, 1, D_RNN), mixer_seq),
                 pl.BlockSpec((1, SUBLANES, D_FF), ffn_seq),
                 pl.BlockSpec((1, D_XA, N_MEM), mixer_seq),
                 pl.BlockSpec((1, D_XA, N_MEM), mixer_seq))
    scratch = [pltpu.VMEM((POOL_CTX_ROWS, D_POOL), F32),
               pltpu.VMEM((SUBLANES, D_RNN), F32),
               pltpu.VMEM((1, D_RNN), F32),
               pltpu.VMEM((SUBLANES, D_FF), F32),
               pltpu.VMEM((D_XA, XA_HEADS * N_MEM), BF16),
               pltpu.VMEM((XA_HEADS * N_MEM, D_XA), BF16),
               pltpu.VMEM((TOKEN_TILE, D_MODEL), F32),
               pltpu.VMEM((TOKEN_TILE, D_MODEL), BF16)]
    return pl.pallas_call(
        functools.partial(_prompt_kernel, nt=nt, n_tiles=n_tiles),
        grid=(n_tiles + 1,),
        in_specs=in_specs,
        out_specs=out_specs,
        out_shape=out_shape,
        scratch_shapes=scratch,
        compiler_params=pltpu.CompilerParams(
            dimension_semantics=("arbitrary",), vmem_limit_bytes=VMEM_LIMIT),
        name="prompt_layer",
    )(x, mem, *consts)


def _sample_mixer_kernel(xs_ref, spool_ref, sconv_ref, sh_ref, ckt_ref, cvt_ref,
                         g1_ref, win_ref, wpool_ref, pscale_ref, cw_ref, cb_ref,
                         wrg_ref, brg_ref, lam_ref,
                         mix_ref, npool_ref, nconv_ref, nh_ref,
                         q_scr, o_scr):
    i = pl.program_id(0)
    nb, nt, _ = xs_ref.shape

    @pl.when(i == 0)
    def _dense():
        x = jnp.concatenate([xs_ref[:, t, :] for t in range(nt)], axis=0)
        proj = _dot(_rms(x, g1_ref[...]), win_ref[...])
        u_pool = proj[:, :D_POOL]
        x_rnn = proj[:, D_POOL:D_POOL + D_RNN]
        g_rnn = proj[:, D_POOL + D_RNN:D_POOL + 2 * D_RNN]
        q = proj[:, D_POOL + 2 * D_RNN:] * (1.0 / math.sqrt(XA_HEAD_DIM))
        rows = lambda v, t: v[t * nb:(t + 1) * nb]

        full = [spool_ref[j] for j in range(POOL_CTX)] + [rows(u_pool, t) for t in range(nt)]
        n = len(full)
        p2 = {j: full[j] + full[j - 1] for j in range(1, n)}
        p4 = {j: p2[j] + p2[j - 2] for j in range(3, n)}
        p8 = {j: p4[j] + p4[j - 4] for j in range(7, n)}
        p16 = {j: p8[j] + p8[j - 8] for j in range(15, n)}
        win = _pool_window_lane((nb, D_POOL))
        diffs = []
        for t in range(nt):
            j = POOL_CTX + t
            cnt = jnp.minimum(PAST_LEN + t + 1, win).astype(F32)
            diffs.append(_pool_select(p2[j], p4[j], p8[j], p16[j]) / cnt - full[j])
        o_pool = _dot(jnp.concatenate(diffs, axis=0), wpool_ref[...]) * pscale_ref[...]
        for j in range(POOL_CTX):
            npool_ref[j] = full[nt + j]

        fullc = [sconv_ref[j] for j in range(RNN_CONV - 1)] + [rows(x_rnn, t) for t in range(nt)]
        cw = cw_ref[...]
        xcs = []
        for t in range(nt):
            acc = cb_ref[...] + cw[0:1] * fullc[t]
            for k in range(1, RNN_CONV):
                acc = acc + cw[k:k + 1] * fullc[t + k]
            xcs.append(acc)
        for j in range(RNN_CONV - 1):
            nconv_ref[j] = fullc[nt + j]
        xc = jnp.concatenate(xcs, axis=0)
        r_half, i_half = _gate_dots(xc, wrg_ref, brg_ref)
        a, bt = _rg_coeffs(xc, r_half, i_half, lam_ref[...])
        h = sh_ref[...]
        hs = []
        for t in range(nt):
            h = rows(a, t) * h + rows(bt, t)
            hs.append(h)
        nh_ref[...] = h
        o_rnn = _gelu_tanh(g_rnn) * jnp.concatenate(hs, axis=0)

        mix_ref[:, :D_POOL] = _group_rms(o_pool)
        mix_ref[:, D_POOL:D_POOL + D_RNN] = _group_rms(o_rnn)
        q_scr[...] = jnp.zeros_like(q_scr)
        for t in range(nt):
            for half in range(XA_LANE_HALVES):
                q_scr[half, pl.ds(t, nb, stride=SUBLANES), :] = (
                    rows(q, t)[:, half * LANES:(half + 1) * LANES])

    for j in range(SEQ_CHUNK):
        seq_rows = pl.ds(pl.multiple_of((i * SEQ_CHUNK + j) * SUBLANES, SUBLANES), SUBLANES)
        qb = jnp.concatenate([q_scr[half, seq_rows, :] for half in range(XA_LANE_HALVES)],
                             axis=1)
        qbd = jnp.concatenate(
            [jnp.where(_head_mask(qb.shape, hd), qb, 0.0) for hd in range(XA_HEADS)], axis=0)
        ktb = ckt_ref[j].astype(BF16)
        vtb = cvt_ref[j].astype(BF16)
        p = _softmax_rows(jnp.dot(qbd.astype(BF16), ktb, preferred_element_type=F32))
        pv = lax.dot_general(p.astype(BF16), vtb, (((1,), (1,)), ((), ())),
                             preferred_element_type=F32)
        ob = jnp.zeros((SUBLANES, D_XA), F32)
        for hd in range(XA_HEADS):
            part = pv[hd * SUBLANES:(hd + 1) * SUBLANES]
            ob = jnp.where(_head_mask(part.shape, hd), part, ob)
        for half in range(XA_LANE_HALVES):
            o_scr[half, seq_rows, :] = ob[:, half * LANES:(half + 1) * LANES]

    @pl.when(i == pl.num_programs(0) - 1)
    def _finish():
        for t in range(nt):
            o_t = jnp.concatenate([o_scr[half, pl.ds(t, nb, stride=SUBLANES), :]
                                   for half in range(XA_LANE_HALVES)], axis=1)
            mix_ref[t * nb:(t + 1) * nb, D_POOL + D_RNN:] = _group_rms(o_t)


def _sample_mixer(xs, spool, sconv, sh, ckt, cvt, w):
    nb, nt, _ = xs.shape
    consts_in = [xs, spool, sconv, sh]
    consts_w = [w['g_mix_norm'], w['w_in'], w['w_pool'], w['pool_scale'], w['rnn_conv_w'],
                w['rnn_conv_b'], w['w_rg'], w['b_rg'], w['rg_lambda']]
    kv_spec = pl.BlockSpec((SEQ_CHUNK, D_XA, N_MEM), lambda i: (i, 0, 0))
    in_specs = ([_const_spec(c.shape) for c in consts_in] + [kv_spec, kv_spec]
                + [_const_spec(c.shape) for c in consts_w])
    out_shape = (jax.ShapeDtypeStruct((nt * nb, D_MODEL), F32),
                 jax.ShapeDtypeStruct(spool.shape, F32),
                 jax.ShapeDtypeStruct(sconv.shape, F32),
                 jax.ShapeDtypeStruct(sh.shape, F32))
    out_specs = tuple(pl.BlockSpec(s.shape, lambda i, n=len(s.shape): (0,) * n) for s in out_shape)
    return pl.pallas_call(
        _sample_mixer_kernel,
        grid=(nb // SEQ_CHUNK,),
        in_specs=in_specs,
        out_specs=out_specs,
        out_shape=out_shape,
        scratch_shapes=[pltpu.VMEM((XA_LANE_HALVES, nb * SUBLANES, LANES), F32),
                        pltpu.VMEM((XA_LANE_HALVES, nb * SUBLANES, LANES), F32)],
        compiler_params=pltpu.CompilerParams(
            dimension_semantics=("arbitrary",), vmem_limit_bytes=VMEM_LIMIT),
        name="sample_mixer",
    )(*consts_in, ckt, cvt, *consts_w)


def _sample_ffn_kernel(xs_ref, mix_ref, sffn_ref, gout_ref, wout_ref, g2_ref, wfg_ref, wfu_ref,
                       fcw_ref, fcb_ref, wfd_ref, gfin_ref, y_ref, nffn_ref,
                       x1_scr, xn2_scr, acc_scr):
    c = pl.program_id(0)
    nb, nt, _ = xs_ref.shape

    @pl.when(c == 0)
    def _out_projection():
        x = jnp.concatenate([xs_ref[:, t, :] for t in range(nt)], axis=0)
        x1 = x + _dot(mix_ref[...] * gout_ref[...], wout_ref[...])
        x1_scr[...] = x1
        xn2_scr[...] = _rms(x1, g2_ref[...]).astype(BF16)
        acc_scr[...] = jnp.zeros_like(acc_scr)

    xn2 = xn2_scr[...]
    gate = jnp.dot(xn2, wfg_ref[0], preferred_element_type=F32)
    up_half = jnp.dot(xn2, wfu_ref[0], preferred_element_type=F32)
    full = [sffn_ref[:, j, :] for j in range(FFN_CONV - 1)]
    full += [gate[t * nb:(t + 1) * nb] for t in range(nt)]
    fw = fcw_ref[...]
    convs = []
    for t in range(nt):
        a = fcb_ref[...] + fw[0:1] * full[t]
        for k in range(1, FFN_CONV):
            a = a + fw[k:k + 1] * full[t + k]
        convs.append(a)
    for j in range(FFN_CONV - 1):
        nffn_ref[:, j, :] = full[nt + j]
    act = (_gelu_tanh_x2(jnp.concatenate(convs, axis=0)) * up_half).astype(BF16)
    acc_scr[...] += jnp.dot(act, wfd_ref[...], preferred_element_type=F32)

    @pl.when(c == pl.num_programs(0) - 1)
    def _final_norm():
        y = _rms(x1_scr[...] + acc_scr[...], gfin_ref[...])
        for t in range(nt):
            y_ref[:, t, :] = y[t * nb:(t + 1) * nb]


def _sample_ffn(xs, mix, sffn, w):
    nb, taps, _ = sffn.shape
    rows = mix.shape[0]
    ins = [xs, mix, sffn, w['g_mix_out'], w['w_out'], w['g_ffn_norm'], w['w_ff_gate'],
           w['w_ff_up_half'], w['ffn_conv_w'], w['ffn_conv_b'], w['w_ff_down'], w['g_final']]
    chunk_cols = lambda c: (0, c)
    in_specs = [_const_spec(xs.shape), _const_spec(mix.shape),
                pl.BlockSpec((nb, taps, FF_CHUNK), lambda c: (0, 0, c)),
                _const_spec(w['g_mix_out'].shape), _const_spec(w['w_out'].shape),
                _const_spec(w['g_ffn_norm'].shape),
                pl.BlockSpec((1, D_MODEL, FF_CHUNK), lambda c: (c, 0, 0)),
                pl.BlockSpec((1, D_MODEL, FF_CHUNK), lambda c: (c, 0, 0)),
                pl.BlockSpec((FFN_CONV, FF_CHUNK), chunk_cols),
                pl.BlockSpec((1, FF_CHUNK), chunk_cols),
                pl.BlockSpec((FF_CHUNK, D_MODEL), lambda c: (c, 0)),
                _const_spec(w['g_final'].shape)]
    out_shape = (jax.ShapeDtypeStruct(xs.shape, F32), jax.ShapeDtypeStruct(sffn.shape, F32))
    out_specs = (pl.BlockSpec(xs.shape, lambda c: (0, 0, 0)),
                 pl.BlockSpec((nb, taps, FF_CHUNK), lambda c: (0, 0, c)))
    return pl.pallas_call(
        _sample_ffn_kernel,
        grid=(FF_CHUNKS,),
        in_specs=in_specs,
        out_specs=out_specs,
        out_shape=out_shape,
        scratch_shapes=[pltpu.VMEM((rows, D_MODEL), F32),
                        pltpu.VMEM((rows, D_MODEL), BF16),
                        pltpu.VMEM((rows, D_MODEL), F32)],
        compiler_params=pltpu.CompilerParams(
            dimension_semantics=("arbitrary",), vmem_limit_bytes=VMEM_LIMIT),
        name="sample_ffn",
    )(*ins)


def _cast_ffn_kernel(g_ref, u_ref, d_ref, go_ref, uo_ref, do_ref):
    go_ref[0] = g_ref[...].astype(BF16)
    uo_ref[0] = (0.5 * u_ref[...]).astype(BF16)
    do_ref[...] = d_ref[...].astype(BF16)


def _cast_ffn_weights(w_gate, w_up, w_down):
    col_in = pl.BlockSpec((D_MODEL, FF_CHUNK), lambda c: (0, c))
    row_blk = pl.BlockSpec((FF_CHUNK, D_MODEL), lambda c: (c, 0))
    slab_out = pl.BlockSpec((1, D_MODEL, FF_CHUNK), lambda c: (c, 0, 0))
    slabs = jax.ShapeDtypeStruct((FF_CHUNKS, D_MODEL, FF_CHUNK), BF16)
    return pl.pallas_call(
        _cast_ffn_kernel,
        grid=(FF_CHUNKS,),
        in_specs=[col_in, col_in, row_blk],
        out_specs=(slab_out, slab_out, row_blk),
        out_shape=(slabs, slabs, jax.ShapeDtypeStruct((D_FF, D_MODEL), BF16)),
        compiler_params=pltpu.CompilerParams(dimension_semantics=("arbitrary",)),
        name="cast_ffn_weights",
    )(w_gate, w_up, w_down)


def _cast_proj_kernel(win_ref, wout_ref, wk_ref, wv_ref, wino_ref, wouto_ref, wkvo_ref):
    wino_ref[...] = win_ref[...].astype(BF16)
    wouto_ref[...] = wout_ref[...].astype(BF16)
    wkvo_ref[:, :D_XA] = wk_ref[...].astype(BF16)
    wkvo_ref[:, D_XA:] = wv_ref[...].astype(BF16)


def _cast_proj_weights(w_in, w_out, w_mem_k, w_mem_v):
    rows = MXU_DIM
    blk = lambda n: pl.BlockSpec((rows, n), lambda r: (r, 0))
    return pl.pallas_call(
        _cast_proj_kernel,
        grid=(D_MODEL // rows,),
        in_specs=[blk(D_IN), blk(D_MODEL), blk(D_XA), blk(D_XA)],
        out_specs=(blk(D_IN), blk(D_MODEL), blk(2 * D_XA)),
        out_shape=(jax.ShapeDtypeStruct((D_MODEL, D_IN), BF16),
                   jax.ShapeDtypeStruct((D_MODEL, D_MODEL), BF16),
                   jax.ShapeDtypeStruct((D_MODEL, 2 * D_XA), BF16)),
        compiler_params=pltpu.CompilerParams(dimension_semantics=("arbitrary",)),
        name="cast_proj_weights",
    )(w_in, w_out, w_mem_k, w_mem_v)


def _block_diag(blocks):
    n, c, d = blocks.shape
    eye = jnp.eye(n, dtype=blocks.dtype)
    return (eye[:, None, :, None] * blocks[:, :, None, :]).reshape(n * c, n * d)


def _diag_tiles(blocks):
    bd = _block_diag(blocks)
    return jnp.stack([bd[j * MXU_DIM:(j + 1) * MXU_DIM, j * MXU_DIM:(j + 1) * MXU_DIM]
                      for j in range(bd.shape[0] // MXU_DIM)])


def _layer_weights(l, g_mix_norm, w_in, w_pool, pool_scale, rnn_conv_w, rnn_conv_b, w_rg_a,
                   b_rg_a, w_rg_x, b_rg_x, rg_lambda, g_mem_norm, w_mem_k, w_mem_v, g_mix_out,
                   w_out, g_ffn_norm, w_ff_gate, w_ff_up, ffn_conv_w, ffn_conv_b, w_ff_down,
                   g_final):
    row = lambda v: v.reshape(1, -1)
    wfg, wfu_half, wfd = _cast_ffn_weights(w_ff_gate[l], w_ff_up[l], w_ff_down[l])
    w_in_b, w_out_b, w_kv_b = _cast_proj_weights(w_in[l], w_out[l], w_mem_k[l], w_mem_v[l])
    return {
        'g_mix_norm': row(g_mix_norm[l]),
        'w_in': w_in_b,
        'w_pool': _block_diag(w_pool[l]).astype(BF16),
        'pool_scale': row(pool_scale[l]),
        'rnn_conv_w': rnn_conv_w[l],
        'rnn_conv_b': row(rnn_conv_b[l]),
        'w_rg': (0.5 * jnp.stack([_diag_tiles(w_rg_a[l]), _diag_tiles(w_rg_x[l])])).astype(BF16),
        'b_rg': 0.5 * jnp.stack([b_rg_a[l], b_rg_x[l]]),
        'rg_lambda': row(rg_lambda[l]),
        'g_mem': row(g_mem_norm[l]),
        'w_kv': w_kv_b,
        'g_mix_out': row(g_mix_out[l]),
        'w_out': w_out_b,
        'g_ffn_norm': row(g_ffn_norm[l]),
        'w_ff_gate': wfg,
        'w_ff_up_half': wfu_half,
        'ffn_conv_w': ffn_conv_w[l],
        'ffn_conv_b': row(ffn_conv_b[l]),
        'w_ff_down': wfd,
        'g_final': row(g_final),
    }


def _mem_major(cache):
    b = cache.shape[0]
    return jnp.transpose(cache, (0, 2, 3, 1)).reshape(b, D_XA, N_MEM)


def _mem_minor(kt):
    b = kt.shape[0]
    return jnp.transpose(kt.reshape(b, XA_HEADS, XA_HEAD_DIM, N_MEM), (0, 3, 1, 2))


def kernel(x_prompt, x_sample, mem_prompt, state_pool, state_rnn_conv, state_rnn_h, state_ffn_conv, cache_mem_k, cache_mem_v, g_mix_norm, w_in, w_pool, pool_scale, rnn_conv_w, rnn_conv_b, w_rg_a, b_rg_a, w_rg_x, b_rg_x, rg_lambda, g_mem_norm, w_mem_k, w_mem_v, g_mix_out, w_out, g_ffn_norm, w_ff_gate, w_ff_up, ffn_conv_w, ffn_conv_b, w_ff_down, g_final):
    depth = w_in.shape[0]
    assert depth == 1, "the final norm is fused into the single layer"
    bp = x_prompt.shape[0]
    w = _layer_weights(0, g_mix_norm, w_in, w_pool, pool_scale, rnn_conv_w, rnn_conv_b, w_rg_a,
                       b_rg_a, w_rg_x, b_rg_x, rg_lambda, g_mem_norm, w_mem_k, w_mem_v,
                       g_mix_out, w_out, g_ffn_norm, w_ff_gate, w_ff_up, ffn_conv_w, ffn_conv_b,
                       w_ff_down, g_final)

    y_p, pool_p, conv_p, h_p, ffn_p, mkt, mvt = _prompt_layer(x_prompt, mem_prompt, w)

    spool = jnp.transpose(state_pool[0], (1, 0, 2))
    sconv = jnp.transpose(state_rnn_conv[0], (1, 0, 2))
    mix, pool_s, conv_s, h_s = _sample_mixer(x_sample, spool, sconv, state_rnn_h[0],
                                             _mem_major(cache_mem_k[0]), _mem_major(cache_mem_v[0]), w)
    y_s, ffn_s = _sample_ffn(x_sample, mix, state_ffn_conv[0], w)

    return (y_p,
            y_s,
            pool_p[:, POOL_CTX_ROWS - POOL_CTX:][None],
            conv_p[:, SUBLANES - (RNN_CONV - 1):][None],
            h_p.reshape(1, bp, D_RNN),
            ffn_p[:, SUBLANES - (FFN_CONV - 1):][None],
            _mem_minor(mkt)[None],
            _mem_minor(mvt)[None],
            jnp.transpose(pool_s, (1, 0, 2))[None],
            jnp.transpose(conv_s, (1, 0, 2))[None],
            h_s[None],
            ffn_s[None])
```

```python
import functools
import math

import jax
import jax.numpy as jnp
from jax import lax
from jax.experimental import pallas as pl
from jax.experimental.pallas import tpu as pltpu

D_MODEL = 1024
PAST_LEN = 16384
D_POOL = D_MODEL // 4
POOL_WINDOWS = (2, 4, 8, 16)
POOL_GW = D_POOL // len(POOL_WINDOWS)
POOL_CTX = max(POOL_WINDOWS) - 1
D_RNN = D_MODEL // 2
RG_BLOCKS = 8
RNN_CONV = 4
RG_C = 8.0
XA_HEADS = 4
XA_HEAD_DIM = 64
D_XA = XA_HEADS * XA_HEAD_DIM
D_IN = D_POOL + 2 * D_RNN + D_XA
N_MEM = 256
D_FF = 3 * D_MODEL
FFN_CONV = 3
EPS = 1e-6

SUBLANES = 8
LANES = 128
MXU_DIM = 256
XA_LANE_HALVES = D_XA // LANES
RG_TILES = D_RNN // MXU_DIM
POOL_CTX_ROWS = 16
TOKEN_TILE = 512
FF_CHUNK = 512
FF_CHUNKS = D_FF // FF_CHUNK
SEQ_CHUNK = 16
VMEM_LIMIT = 60 * 1024 * 1024

BF16 = jnp.bfloat16
F32 = jnp.float32


def _rms(x, g):
    ms = jnp.mean(x * x, axis=-1, keepdims=True)
    return x * lax.rsqrt(ms + EPS) * g


def _group_rms(x):
    ms = jnp.mean(x * x, axis=-1, keepdims=True)
    return x * lax.rsqrt(ms + EPS)


def _gelu_tanh(x):
    c = math.sqrt(2.0 / math.pi)
    half = 0.5 * x
    return half * jnp.tanh(x * (c + (c * 0.044715) * (x * x))) + half


def _gelu_tanh_x2(x):
    c = math.sqrt(2.0 / math.pi)
    return x * jnp.tanh(x * (c + (c * 0.044715) * (x * x))) + x


def _softplus(z):
    return jnp.maximum(z, 0.0) + jnp.log1p(jnp.exp(-jnp.abs(z)))


def _dot(a, b):
    return jnp.dot(a.astype(BF16), b, preferred_element_type=F32)


def _gate_dots(xc, wrg_ref, brg_ref):
    out = []
    for gate in range(2):
        parts = [_dot(xc[:, j * MXU_DIM:(j + 1) * MXU_DIM], wrg_ref[gate, j])
                 for j in range(RG_TILES)]
        out.append(jnp.concatenate(parts, axis=1) + brg_ref[gate:gate + 1])
    return out


def _rg_coeffs(xc, r_half, i_half, lam):
    half_c = (-0.5 * RG_C) * _softplus(-lam)
    log_a = half_c * jnp.tanh(r_half) + half_c
    a = jnp.exp(log_a)
    y = jnp.tanh(log_a) * (-1.0 - a * a)
    mult = jnp.where(y > 0.0, y * lax.rsqrt(y), 0.0)
    i = 0.5 * jnp.tanh(i_half) + 0.5
    return a, mult * i * xc


def _pool_select(p2, p4, p8, p16):
    lane = lax.broadcasted_iota(jnp.int32, p2.shape, 1)
    return jnp.where(lane < POOL_GW, p2,
                     jnp.where(lane < 2 * POOL_GW, p4,
                               jnp.where(lane < 3 * POOL_GW, p8, p16)))


def _pool_window_lane(shape):
    lane = lax.broadcasted_iota(jnp.int32, shape, 1)
    return jnp.where(lane < POOL_GW, POOL_WINDOWS[0],
                     jnp.where(lane < 2 * POOL_GW, POOL_WINDOWS[1],
                               jnp.where(lane < 3 * POOL_GW, POOL_WINDOWS[2], POOL_WINDOWS[3])))


def _head_mask(shape, h):
    lane = lax.broadcasted_iota(jnp.int32, shape, 1)
    return (lane >= h * XA_HEAD_DIM) & (lane < (h + 1) * XA_HEAD_DIM)


def _softmax_rows(s):
    m = jnp.max(s, axis=-1, keepdims=True)
    e = jnp.exp(s - m)
    return e / jnp.sum(e, axis=-1, keepdims=True)


def _shift_rows(ext, k, ctx_rows):
    return pltpu.roll(ext, k, 0)[ctx_rows:]


def _scan_rows(a, b, h_in):
    n, c = a.shape
    groups = n // SUBLANES
    a3 = a.reshape(groups, SUBLANES, c)
    b3 = b.reshape(groups, SUBLANES, c)
    sub = lax.broadcasted_iota(jnp.int32, (1, SUBLANES, c), 1)
    k = 1
    while k < SUBLANES:
        valid = sub >= k
        a_prev = jnp.where(valid, pltpu.roll(a3, k, 1), 1.0)
        b_prev = jnp.where(valid, pltpu.roll(b3, k, 1), 0.0)
        b3 = a3 * b_prev + b3
        a3 = a3 * a_prev
        k *= 2
    carry = h_in
    hs = []
    for g in range(groups):
        h = a3[g] * carry + b3[g]
        hs.append(h)
        carry = h[SUBLANES - 1:SUBLANES]
    return jnp.concatenate(hs, axis=0)


def _ffn_chunk(xn2, c, wfg_ref, wfu_ref, fcw_ref, fcb_ref, ffn_ctx, offn_ref):
    tt = xn2.shape[0]
    cols = slice(c * FF_CHUNK, (c + 1) * FF_CHUNK)
    gate = jnp.dot(xn2, wfg_ref[...], preferred_element_type=F32)
    up_half = jnp.dot(xn2, wfu_ref[...], preferred_element_type=F32)
    ext = jnp.concatenate([ffn_ctx[:, cols], gate], axis=0)
    fw = fcw_ref[:, cols]
    gate_c = fcb_ref[:, cols] + fw[FFN_CONV - 1:FFN_CONV] * gate
    for k in range(1, FFN_CONV):
        gate_c = gate_c + fw[FFN_CONV - 1 - k:FFN_CONV - k] * _shift_rows(ext, k, SUBLANES)
    new_ffn = ext[tt:]
    ffn_ctx[:, cols] = new_ffn
    offn_ref[0, :, cols] = new_ffn
    return (_gelu_tanh_x2(gate_c) * up_half).astype(BF16)


def _prompt_kernel(*refs, nt, n_tiles):
    (x_ref, mem_ref, gmem_ref, wkv_ref, g1_ref, win_ref, wpool_ref, pscale_ref, cw_ref, cb_ref,
     wrg_ref, brg_ref, lam_ref, gout_ref, wout_ref, g2_ref, wfg_ref, wfu_ref,
     fcw_ref, fcb_ref, wfd_ref, gfin_ref,
     y_ref, opool_ref, oconv_ref, oh_ref, offn_ref, mk_ref, mv_ref,
     pool_ctx, conv_ctx, h_ctx, ffn_ctx, kt_scr, v_scr, act_scr, x1_scr, xn2_scr) = refs
    wfg_refs = [wfg_ref.at[c] for c in range(FF_CHUNKS)]
    wfu_refs = [wfu_ref.at[c] for c in range(FF_CHUNKS)]
    s = pl.program_id(0)
    tm = lax.rem(jnp.minimum(s, n_tiles - 1), nt)
    tf = lax.rem(jnp.maximum(s - 1, 0), nt)
    tt = x_ref.shape[1]

    @pl.when(s == 0)
    def _no_previous_tile():
        x1_scr[...] = jnp.zeros_like(x1_scr)
        xn2_scr[...] = jnp.zeros_like(xn2_scr)

    @pl.when(tm == 0)
    def _start_sequence():
        pool_ctx[...] = jnp.zeros_like(pool_ctx)
        conv_ctx[...] = jnp.zeros_like(conv_ctx)
        h_ctx[...] = jnp.zeros_like(h_ctx)
        mn = _rms(mem_ref[0], gmem_ref[...])
        kv = _dot(mn, wkv_ref[...])
        kt = kv[:, :D_XA].T
        mk_ref[0] = kt
        mv_ref[0] = kv[:, D_XA:].T
        v = kv[:, D_XA:]
        chan = lax.broadcasted_iota(jnp.int32, kt.shape, 0)
        for hd in range(XA_HEADS):
            own = (chan >= hd * XA_HEAD_DIM) & (chan < (hd + 1) * XA_HEAD_DIM)
            kt_scr[:, hd * N_MEM:(hd + 1) * N_MEM] = jnp.where(own, kt, 0.0).astype(BF16)
            v_scr[hd * N_MEM:(hd + 1) * N_MEM, :] = (
                jnp.where(_head_mask(v.shape, hd), v, 0.0).astype(BF16))

    @pl.when(tf == 0)
    def _start_ffn_sequence():
        ffn_ctx[...] = jnp.zeros_like(ffn_ctx)

    xn2_prev = xn2_scr[...]
    def ffn(c):
        act_scr[:, c * FF_CHUNK:(c + 1) * FF_CHUNK] = _ffn_chunk(
            xn2_prev, c, wfg_refs[c], wfu_refs[c], fcw_ref, fcb_ref, ffn_ctx, offn_ref)

    def head_probs(hd):
        return _softmax_rows(scores[:, hd * N_MEM:(hd + 1) * N_MEM]).astype(BF16)

    ffn(0)
    x = x_ref[0]
    proj = _dot(_rms(x, g1_ref[...]), win_ref[...])
    u_pool = proj[:, :D_POOL]
    x_rnn = proj[:, D_POOL:D_POOL + D_RNN]
    g_rnn = proj[:, D_POOL + D_RNN:D_POOL + 2 * D_RNN]
    q = proj[:, D_POOL + 2 * D_RNN:]

    ext = jnp.concatenate([pool_ctx[...], u_pool], axis=0)
    p2 = ext + pltpu.roll(ext, 1, 0)
    p4 = p2 + pltpu.roll(p2, 2, 0)
    p8 = p4 + pltpu.roll(p4, 4, 0)
    p16 = p8 + pltpu.roll(p8, 8, 0)
    sm = _pool_select(p2, p4, p8, p16)[POOL_CTX_ROWS:]
    head = (POOL_CTX_ROWS, D_POOL)
    win = _pool_window_lane(head)
    pos = tm * tt + lax.broadcasted_iota(jnp.int32, head, 0)
    inv_head = 1.0 / jnp.minimum(pos + 1, win).astype(F32)
    inv_rest = 1.0 / win[:1].astype(F32)
    diff = jnp.concatenate([sm[:POOL_CTX_ROWS] * inv_head, sm[POOL_CTX_ROWS:] * inv_rest],
                           axis=0) - u_pool
    o_pool = _dot(diff, wpool_ref[...]) * pscale_ref[...]
    new_pool = ext[tt:]
    pool_ctx[...] = new_pool

    ext = jnp.concatenate([conv_ctx[...], x_rnn], axis=0)
    cw = cw_ref[...]
    xc = cb_ref[...] + cw[RNN_CONV - 1:RNN_CONV] * x_rnn
    for k in range(1, RNN_CONV):
        xc = xc + cw[RNN_CONV - 1 - k:RNN_CONV - k] * _shift_rows(ext, k, SUBLANES)
    new_conv = ext[tt:]
    conv_ctx[...] = new_conv

    ffn(1)
    r_half, i_half = _gate_dots(xc, wrg_ref, brg_ref)
    a, bt = _rg_coeffs(xc, r_half, i_half, lam_ref[...])
    scores = _dot(q * (1.0 / math.sqrt(XA_HEAD_DIM)), kt_scr[...])

    probs = []
    for hd in range(XA_HEADS):
        ffn(2 + hd)
        probs.append(head_probs(hd))
    o_xa = jnp.dot(jnp.concatenate(probs, axis=1), v_scr[...], preferred_element_type=F32)

    x2 = x1_scr[...] + jnp.dot(act_scr[...], wfd_ref[...], preferred_element_type=F32)
    h = _scan_rows(a, bt, h_ctx[...])
    h_last = h[tt - 1:tt]
    h_ctx[...] = h_last
    o_rnn = _gelu_tanh(g_rnn) * h
    mix = jnp.concatenate([_group_rms(o_pool), _group_rms(o_rnn), _group_rms(o_xa)], axis=-1)

    x1 = x + _dot(mix * gout_ref[...], wout_ref[...])
    y_ref[0] = _rms(x2, gfin_ref[...])
    xn2 = _rms(x1, g2_ref[...]).astype(BF16)

    x1_scr[...] = x1
    xn2_scr[...] = xn2

    @pl.when(s < n_tiles)
    def _emit_mixer_state():
        opool_ref[0] = new_pool
        oconv_ref[0] = new_conv
        oh_ref[0] = h_last


def _const_spec(shape):
    zeros = (0,) * len(shape)
    return pl.BlockSpec(shape, lambda *_: zeros, pipeline_mode=pl.Buffered(1))


def _prompt_layer(x, mem, w):
    batch, seq, _ = x.shape
    nt = seq // TOKEN_TILE
    n_tiles = batch * nt
    consts = [w['g_mem'], w['w_kv'], w['g_mix_norm'], w['w_in'], w['w_pool'], w['pool_scale'],
              w['rnn_conv_w'], w['rnn_conv_b'], w['w_rg'], w['b_rg'], w['rg_lambda'],
              w['g_mix_out'], w['w_out'], w['g_ffn_norm'], w['w_ff_gate'], w['w_ff_up_half'],
              w['ffn_conv_w'], w['ffn_conv_b'], w['w_ff_down'], w['g_final']]
    mixer_tile = lambda s: jnp.minimum(s, n_tiles - 1)
    ffn_tile = lambda s: jnp.maximum(s - 1, 0)
    mixer_seq = lambda s: (mixer_tile(s) // nt, 0, 0)
    ffn_seq = lambda s: (ffn_tile(s) // nt, 0, 0)
    in_specs = [pl.BlockSpec((1, TOKEN_TILE, D_MODEL),
                             lambda s: (mixer_tile(s) // nt, mixer_tile(s) % nt, 0)),
                pl.BlockSpec((1, N_MEM, D_MODEL), mixer_seq, pipeline_mode=pl.Buffered(1))]
    in_specs += [_const_spec(c.shape) for c in consts]
    out_shape = (jax.ShapeDtypeStruct((batch, seq, D_MODEL), F32),
                 jax.ShapeDtypeStruct((batch, POOL_CTX_ROWS, D_POOL), F32),
                 jax.ShapeDtypeStruct((batch, SUBLANES, D_RNN), F32),
                 jax.ShapeDtypeStruct((batch, 1, D_RNN), F32),
                 jax.ShapeDtypeStruct((batch, SUBLANES, D_FF), F32),
                 jax.ShapeDtypeStruct((batch, D_XA, N_MEM), F32),
                 jax.ShapeDtypeStruct((batch, D_XA, N_MEM), F32))
    out_specs = (pl.BlockSpec((1, TOKEN_TILE, D_MODEL),
                              lambda s: (ffn_tile(s) // nt, ffn_tile(s) % nt, 0)),
                 pl.BlockSpec((1, POOL_CTX_ROWS, D_POOL), mixer_seq),
                 pl.BlockSpec((1, SUBLANES, D_RNN), mixer_seq),
                 pl.BlockSpec((1, 1, D_RNN), mixer_seq),
                 pl.BlockSpec((1, SUBLANES, D_FF), ffn_seq),
                 pl.BlockSpec((1, D_XA, N_MEM), mixer_seq),
                 pl.BlockSpec((1, D_XA, N_MEM), mixer_seq))
    scratch = [pltpu.VMEM((POOL_CTX_ROWS, D_POOL), F32),
               pltpu.VMEM((SUBLANES, D_RNN), F32),
               pltpu.VMEM((1, D_RNN), F32),
               pltpu.VMEM((SUBLANES, D_FF), F32),
               pltpu.VMEM((D_XA, XA_HEADS * N_MEM), BF16),
               pltpu.VMEM((XA_HEADS * N_MEM, D_XA), BF16),
               pltpu.VMEM((TOKEN_TILE, D_FF), BF16),
               pltpu.VMEM((TOKEN_TILE, D_MODEL), F32),
               pltpu.VMEM((TOKEN_TILE, D_MODEL), BF16)]
    return pl.pallas_call(
        functools.partial(_prompt_kernel, nt=nt, n_tiles=n_tiles),
        grid=(n_tiles + 1,),
        in_specs=in_specs,
        out_specs=out_specs,
        out_shape=out_shape,
        scratch_shapes=scratch,
        compiler_params=pltpu.CompilerParams(
            dimension_semantics=("arbitrary",), vmem_limit_bytes=VMEM_LIMIT),
        name="prompt_layer",
    )(x, mem, *consts)


def _sample_mixer_kernel(xs_ref, spool_ref, sconv_ref, sh_ref, ckt_ref, cvt_ref,
                         g1_ref, win_ref, wpool_ref, pscale_ref, cw_ref, cb_ref,
                         wrg_ref, brg_ref, lam_ref,
                         mix_ref, npool_ref, nconv_ref, nh_ref,
                         q_scr, o_scr):
    i = pl.program_id(0)
    nb, nt, _ = xs_ref.shape

    @pl.when(i == 0)
    def _dense():
        x = jnp.concatenate([xs_ref[:, t, :] for t in range(nt)], axis=0)
        proj = _dot(_rms(x, g1_ref[...]), win_ref[...])
        u_pool = proj[:, :D_POOL]
        x_rnn = proj[:, D_POOL:D_POOL + D_RNN]
        g_rnn = proj[:, D_POOL + D_RNN:D_POOL + 2 * D_RNN]
        q = proj[:, D_POOL + 2 * D_RNN:] * (1.0 / math.sqrt(XA_HEAD_DIM))
        rows = lambda v, t: v[t * nb:(t + 1) * nb]

        full = [spool_ref[j] for j in range(POOL_CTX)] + [rows(u_pool, t) for t in range(nt)]
        n = len(full)
        p2 = {j: full[j] + full[j - 1] for j in range(1, n)}
        p4 = {j: p2[j] + p2[j - 2] for j in range(3, n)}
        p8 = {j: p4[j] + p4[j - 4] for j in range(7, n)}
        p16 = {j: p8[j] + p8[j - 8] for j in range(15, n)}
        win = _pool_window_lane((nb, D_POOL))
        diffs = []
        for t in range(nt):
            j = POOL_CTX + t
            cnt = jnp.minimum(PAST_LEN + t + 1, win).astype(F32)
            diffs.append(_pool_select(p2[j], p4[j], p8[j], p16[j]) / cnt - full[j])
        o_pool = _dot(jnp.concatenate(diffs, axis=0), wpool_ref[...]) * pscale_ref[...]
        for j in range(POOL_CTX):
            npool_ref[j] = full[nt + j]

        fullc = [sconv_ref[j] for j in range(RNN_CONV - 1)] + [rows(x_rnn, t) for t in range(nt)]
        cw = cw_ref[...]
        xcs = []
        for t in range(nt):
            acc = cb_ref[...] + cw[0:1] * fullc[t]
            for k in range(1, RNN_CONV):
                acc = acc + cw[k:k + 1] * fullc[t + k]
            xcs.append(acc)
        for j in range(RNN_CONV - 1):
            nconv_ref[j] = fullc[nt + j]
        xc = jnp.concatenate(xcs, axis=0)
        r_half, i_half = _gate_dots(xc, wrg_ref, brg_ref)
        a, bt = _rg_coeffs(xc, r_half, i_half, lam_ref[...])
        h = sh_ref[...]
        hs = []
        for t in range(nt):
            h = rows(a, t) * h + rows(bt, t)
            hs.append(h)
        nh_ref[...] = h
        o_rnn = _gelu_tanh(g_rnn) * jnp.concatenate(hs, axis=0)

        mix_ref[:, :D_POOL] = _group_rms(o_pool)
        mix_ref[:, D_POOL:D_POOL + D_RNN] = _group_rms(o_rnn)
        q_scr[...] = jnp.zeros_like(q_scr)
        for t in range(nt):
            for half in range(XA_LANE_HALVES):
                q_scr[half, pl.ds(t, nb, stride=SUBLANES), :] = (
                    rows(q, t)[:, half * LANES:(half + 1) * LANES])

    for j in range(SEQ_CHUNK):
        seq_rows = pl.ds(pl.multiple_of((i * SEQ_CHUNK + j) * SUBLANES, SUBLANES), SUBLANES)
        qb = jnp.concatenate([q_scr[half, seq_rows, :] for half in range(XA_LANE_HALVES)],
                             axis=1)
        qbd = jnp.concatenate(
            [jnp.where(_head_mask(qb.shape, hd), qb, 0.0) for hd in range(XA_HEADS)], axis=0)
        ktb = ckt_ref[j].astype(BF16)
        vtb = cvt_ref[j].astype(BF16)
        p = _softmax_rows(jnp.dot(qbd.astype(BF16), ktb, preferred_element_type=F32))
        pv = lax.dot_general(p.astype(BF16), vtb, (((1,), (1,)), ((), ())),
                             preferred_element_type=F32)
        ob = jnp.zeros((SUBLANES, D_XA), F32)
        for hd in range(XA_HEADS):
            part = pv[hd * SUBLANES:(hd + 1) * SUBLANES]
            ob = jnp.where(_head_mask(part.shape, hd), part, ob)
        for half in range(XA_LANE_HALVES):
            o_scr[half, seq_rows, :] = ob[:, half * LANES:(half + 1) * LANES]

    @pl.when(i == pl.num_programs(0) - 1)
    def _finish():
        for t in range(nt):
            o_t = jnp.concatenate([o_scr[half, pl.ds(t, nb, stride=SUBLANES), :]
                                   for half in range(XA_LANE_HALVES)], axis=1)
            mix_ref[t * nb:(t + 1) * nb, D_POOL + D_RNN:] = _group_rms(o_t)


def _sample_mixer(xs, spool, sconv, sh, ckt, cvt, w):
    nb, nt, _ = xs.shape
    consts_in = [xs, spool, sconv, sh]
    consts_w = [w['g_mix_norm'], w['w_in'], w['w_pool'], w['pool_scale'], w['rnn_conv_w'],
                w['rnn_conv_b'], w['w_rg'], w['b_rg'], w['rg_lambda']]
    kv_spec = pl.BlockSpec((SEQ_CHUNK, D_XA, N_MEM), lambda i: (i, 0, 0))
    in_specs = ([_const_spec(c.shape) for c in consts_in] + [kv_spec, kv_spec]
                + [_const_spec(c.shape) for c in consts_w])
    out_shape = (jax.ShapeDtypeStruct((nt * nb, D_MODEL), F32),
                 jax.ShapeDtypeStruct(spool.shape, F32),
                 jax.ShapeDtypeStruct(sconv.shape, F32),
                 jax.ShapeDtypeStruct(sh.shape, F32))
    out_specs = tuple(pl.BlockSpec(s.shape, lambda i, n=len(s.shape): (0,) * n) for s in out_shape)
    return pl.pallas_call(
        _sample_mixer_kernel,
        grid=(nb // SEQ_CHUNK,),
        in_specs=in_specs,
        out_specs=out_specs,
        out_shape=out_shape,
        scratch_shapes=[pltpu.VMEM((XA_LANE_HALVES, nb * SUBLANES, LANES), F32),
                        pltpu.VMEM((XA_LANE_HALVES, nb * SUBLANES, LANES), F32)],
        compiler_params=pltpu.CompilerParams(
            dimension_semantics=("arbitrary",), vmem_limit_bytes=VMEM_LIMIT),
        name="sample_mixer",
    )(*consts_in, ckt, cvt, *consts_w)


def _sample_ffn_kernel(xs_ref, mix_ref, sffn_ref, gout_ref, wout_ref, g2_ref, wfg_ref, wfu_ref,
                       fcw_ref, fcb_ref, wfd_ref, gfin_ref, y_ref, nffn_ref,
                       x1_scr, xn2_scr, acc_scr):
    c = pl.program_id(0)
    nb, nt, _ = xs_ref.shape

    @pl.when(c == 0)
    def _out_projection():
        x = jnp.concatenate([xs_ref[:, t, :] for t in range(nt)], axis=0)
        x1 = x + _dot(mix_ref[...] * gout_ref[...], wout_ref[...])
        x1_scr[...] = x1
        xn2_scr[...] = _rms(x1, g2_ref[...]).astype(BF16)
        acc_scr[...] = jnp.zeros_like(acc_scr)

    xn2 = xn2_scr[...]
    gate = jnp.dot(xn2, wfg_ref[0], preferred_element_type=F32)
    up_half = jnp.dot(xn2, wfu_ref[0], preferred_element_type=F32)
    full = [sffn_ref[:, j, :] for j in range(FFN_CONV - 1)]
    full += [gate[t * nb:(t + 1) * nb] for t in range(nt)]
    fw = fcw_ref[...]
    convs = []
    for t in range(nt):
        a = fcb_ref[...] + fw[0:1] * full[t]
        for k in range(1, FFN_CONV):
            a = a + fw[k:k + 1] * full[t + k]
        convs.append(a)
    for j in range(FFN_CONV - 1):
        nffn_ref[:, j, :] = full[nt + j]
    act = (_gelu_tanh_x2(jnp.concatenate(convs, axis=0)) * up_half).astype(BF16)
    acc_scr[...] += jnp.dot(act, wfd_ref[...], preferred_element_type=F32)

    @pl.when(c == pl.num_programs(0) - 1)
    def _final_norm():
        y = _rms(x1_scr[...] + acc_scr[...], gfin_ref[...])
        for t in range(nt):
            y_ref[:, t, :] = y[t * nb:(t + 1) * nb]


def _sample_ffn(xs, mix, sffn, w):
    nb, taps, _ = sffn.shape
    rows = mix.shape[0]
    ins = [xs, mix, sffn, w['g_mix_out'], w['w_out'], w['g_ffn_norm'], w['w_ff_gate'],
           w['w_ff_up_half'], w['ffn_conv_w'], w['ffn_conv_b'], w['w_ff_down'], w['g_final']]
    chunk_cols = lambda c: (0, c)
    in_specs = [_const_spec(xs.shape), _const_spec(mix.shape),
                pl.BlockSpec((nb, taps, FF_CHUNK), lambda c: (0, 0, c)),
                _const_spec(w['g_mix_out'].shape), _const_spec(w['w_out'].shape),
                _const_spec(w['g_ffn_norm'].shape),
                pl.BlockSpec((1, D_MODEL, FF_CHUNK), lambda c: (c, 0, 0)),
                pl.BlockSpec((1, D_MODEL, FF_CHUNK), lambda c: (c, 0, 0)),
                pl.BlockSpec((FFN_CONV, FF_CHUNK), chunk_cols),
                pl.BlockSpec((1, FF_CHUNK), chunk_cols),
                pl.BlockSpec((FF_CHUNK, D_MODEL), lambda c: (c, 0)),
                _const_spec(w['g_final'].shape)]
    out_shape = (jax.ShapeDtypeStruct(xs.shape, F32), jax.ShapeDtypeStruct(sffn.shape, F32))
    out_specs = (pl.BlockSpec(xs.shape, lambda c: (0, 0, 0)),
                 pl.BlockSpec((nb, taps, FF_CHUNK), lambda c: (0, 0, c)))
    return pl.pallas_call(
        _sample_ffn_kernel,
        grid=(FF_CHUNKS,),
        in_specs=in_specs,
        out_specs=out_specs,
        out_shape=out_shape,
        scratch_shapes=[pltpu.VMEM((rows, D_MODEL), F32),
                        pltpu.VMEM((rows, D_MODEL), BF16),
                        pltpu.VMEM((rows, D_MODEL), F32)],
        compiler_params=pltpu.CompilerParams(
            dimension_semantics=("arbitrary",), vmem_limit_bytes=VMEM_LIMIT),
        name="sample_ffn",
    )(*ins)


def _cast_ffn_kernel(g_ref, u_ref, d_ref, go_ref, uo_ref, do_ref):
    go_ref[0] = g_ref[...].astype(BF16)
    uo_ref[0] = (0.5 * u_ref[...]).astype(BF16)
    do_ref[...] = d_ref[...].astype(BF16)


def _cast_ffn_weights(w_gate, w_up, w_down):
    col_in = pl.BlockSpec((D_MODEL, FF_CHUNK), lambda c: (0, c))
    row_blk = pl.BlockSpec((FF_CHUNK, D_MODEL), lambda c: (c, 0))
    slab_out = pl.BlockSpec((1, D_MODEL, FF_CHUNK), lambda c: (c, 0, 0))
    slabs = jax.ShapeDtypeStruct((FF_CHUNKS, D_MODEL, FF_CHUNK), BF16)
    return pl.pallas_call(
        _cast_ffn_kernel,
        grid=(FF_CHUNKS,),
        in_specs=[col_in, col_in, row_blk],
        out_specs=(slab_out, slab_out, row_blk),
        out_shape=(slabs, slabs, jax.ShapeDtypeStruct((D_FF, D_MODEL), BF16)),
        compiler_params=pltpu.CompilerParams(dimension_semantics=("arbitrary",)),
        name="cast_ffn_weights",
    )(w_gate, w_up, w_down)


def _cast_proj_kernel(win_ref, wout_ref, wk_ref, wv_ref, wino_ref, wouto_ref, wkvo_ref):
    wino_ref[...] = win_ref[...].astype(BF16)
    wouto_ref[...] = wout_ref[...].astype(BF16)
    wkvo_ref[:, :D_XA] = wk_ref[...].astype(BF16)
    wkvo_ref[:, D_XA:] = wv_ref[...].astype(BF16)


def _cast_proj_weights(w_in, w_out, w_mem_k, w_mem_v):
    rows = MXU_DIM
    blk = lambda n: pl.BlockSpec((rows, n), lambda r: (r, 0))
    return pl.pallas_call(
        _cast_proj_kernel,
        grid=(D_MODEL // rows,),
        in_specs=[blk(D_IN), blk(D_MODEL), blk(D_XA), blk(D_XA)],
        out_specs=(blk(D_IN), blk(D_MODEL), blk(2 * D_XA)),
        out_shape=(jax.ShapeDtypeStruct((D_MODEL, D_IN), BF16),
                   jax.ShapeDtypeStruct((D_MODEL, D_MODEL), BF16),
                   jax.ShapeDtypeStruct((D_MODEL, 2 * D_XA), BF16)),
        compiler_params=pltpu.CompilerParams(dimension_semantics=("arbitrary",)),
        name="cast_proj_weights",
    )(w_in, w_out, w_mem_k, w_mem_v)


def _block_diag(blocks):
    n, c, d = blocks.shape
    eye = jnp.eye(n, dtype=blocks.dtype)
    return (eye[:, None, :, None] * blocks[:, :, None, :]).reshape(n * c, n * d)


def _diag_tiles(blocks):
    bd = _block_diag(blocks)
    return jnp.stack([bd[j * MXU_DIM:(j + 1) * MXU_DIM, j * MXU_DIM:(j + 1) * MXU_DIM]
                      for j in range(bd.shape[0] // MXU_DIM)])


def _layer_weights(l, g_mix_norm, w_in, w_pool, pool_scale, rnn_conv_w, rnn_conv_b, w_rg_a,
                   b_rg_a, w_rg_x, b_rg_x, rg_lambda, g_mem_norm, w_mem_k, w_mem_v, g_mix_out,
                   w_out, g_ffn_norm, w_ff_gate, w_ff_up, ffn_conv_w, ffn_conv_b, w_ff_down,
                   g_final):
    row = lambda v: v.reshape(1, -1)
    wfg, wfu_half, wfd = _cast_ffn_weights(w_ff_gate[l], w_ff_up[l], w_ff_down[l])
    w_in_b, w_out_b, w_kv_b = _cast_proj_weights(w_in[l], w_out[l], w_mem_k[l], w_mem_v[l])
    return {
        'g_mix_norm': row(g_mix_norm[l]),
        'w_in': w_in_b,
        'w_pool': _block_diag(w_pool[l]).astype(BF16),
        'pool_scale': row(pool_scale[l]),
        'rnn_conv_w': rnn_conv_w[l],
        'rnn_conv_b': row(rnn_conv_b[l]),
        'w_rg': (0.5 * jnp.stack([_diag_tiles(w_rg_a[l]), _diag_tiles(w_rg_x[l])])).astype(BF16),
        'b_rg': 0.5 * jnp.stack([b_rg_a[l], b_rg_x[l]]),
        'rg_lambda': row(rg_lambda[l]),
        'g_mem': row(g_mem_norm[l]),
        'w_kv': w_kv_b,
        'g_mix_out': row(g_mix_out[l]),
        'w_out': w_out_b,
        'g_ffn_norm': row(g_ffn_norm[l]),
        'w_ff_gate': wfg,
        'w_ff_up_half': wfu_half,
        'ffn_conv_w': ffn_conv_w[l],
        'ffn_conv_b': row(ffn_conv_b[l]),
        'w_ff_down': wfd,
        'g_final': row(g_final),
    }


def _mem_major(cache):
    b = cache.shape[0]
    return jnp.transpose(cache, (0, 2, 3, 1)).reshape(b, D_XA, N_MEM)


def _mem_minor(kt):
    b = kt.shape[0]
    return jnp.transpose(kt.reshape(b, XA_HEADS, XA_HEAD_DIM, N_MEM), (0, 3, 1, 2))


def kernel(x_prompt, x_sample, mem_prompt, state_pool, state_rnn_conv, state_rnn_h, state_ffn_conv, cache_mem_k, cache_mem_v, g_mix_norm, w_in, w_pool, pool_scale, rnn_conv_w, rnn_conv_b, w_rg_a, b_rg_a, w_rg_x, b_rg_x, rg_lambda, g_mem_norm, w_mem_k, w_mem_v, g_mix_out, w_out, g_ffn_norm, w_ff_gate, w_ff_up, ffn_conv_w, ffn_conv_b, w_ff_down, g_final):
    depth = w_in.shape[0]
    assert depth == 1, "the final norm is fused into the single layer"
    bp = x_prompt.shape[0]
    w = _layer_weights(0, g_mix_norm, w_in, w_pool, pool_scale, rnn_conv_w, rnn_conv_b, w_rg_a,
                       b_rg_a, w_rg_x, b_rg_x, rg_lambda, g_mem_norm, w_mem_k, w_mem_v,
                       g_mix_out, w_out, g_ffn_norm, w_ff_gate, w_ff_up, ffn_conv_w, ffn_conv_b,
                       w_ff_down, g_final)

    y_p, pool_p, conv_p, h_p, ffn_p, mkt, mvt = _prompt_layer(x_prompt, mem_prompt, w)

    spool = jnp.transpose(state_pool[0], (1, 0, 2))
    sconv = jnp.transpose(state_rnn_conv[0], (1, 0, 2))
    mix, pool_s, conv_s, h_s = _sample_mixer(x_sample, spool, sconv, state_rnn_h[0],
                                             _mem_major(cache_mem_k[0]), _mem_major(cache_mem_v[0]), w)
    y_s, ffn_s = _sample_ffn(x_sample, mix, state_ffn_conv[0], w)

    return (y_p,
            y_s,
            pool_p[:, POOL_CTX_ROWS - POOL_CTX:][None],
            conv_p[:, SUBLANES - (RNN_CONV - 1):][None],
            h_p.reshape(1, bp, D_RNN),
            ffn_p[:, SUBLANES - (FFN_CONV - 1):][None],
            _mem_minor(mkt)[None],
            _mem_minor(mvt)[None],
            jnp.transpose(pool_s, (1, 0, 2))[None],
            jnp.transpose(conv_s, (1, 0, 2))[None],
            h_s[None],
            ffn_s[None])
```

```python
import functools
import math

import jax
import jax.numpy as jnp
from jax import lax
from jax.experimental import pallas as pl
from jax.experimental.pallas import tpu as pltpu

D_MODEL = 1024
PAST_LEN = 16384
D_POOL = D_MODEL // 4
POOL_WINDOWS = (2, 4, 8, 16)
POOL_GW = D_POOL // len(POOL_WINDOWS)
POOL_CTX = max(POOL_WINDOWS) - 1
D_RNN = D_MODEL // 2
RG_BLOCKS = 8
RNN_CONV = 4
RG_C = 8.0
XA_HEADS = 4
XA_HEAD_DIM = 64
D_XA = XA_HEADS * XA_HEAD_DIM
D_IN = D_POOL + 2 * D_RNN + D_XA
N_MEM = 256
D_FF = 3 * D_MODEL
FFN_CONV = 3
EPS = 1e-6

SUBLANES = 8
LANES = 128
MXU_DIM = 256
XA_LANE_HALVES = D_XA // LANES
RG_TILES = D_RNN // MXU_DIM
POOL_CTX_ROWS = 16
TOKEN_TILE = 512
FF_CHUNK = 512
FF_CHUNKS = D_FF // FF_CHUNK
SEQ_CHUNK = 16
VMEM_LIMIT = 60 * 1024 * 1024

BF16 = jnp.bfloat16
F32 = jnp.float32


def _rms(x, g):
    ms = jnp.mean(x * x, axis=-1, keepdims=True)
    return x * lax.rsqrt(ms + EPS) * g


def _group_rms(x):
    ms = jnp.mean(x * x, axis=-1, keepdims=True)
    return x * lax.rsqrt(ms + EPS)


def _gelu_tanh(x):
    c = math.sqrt(2.0 / math.pi)
    half = 0.5 * x
    return half * jnp.tanh(x * (c + (c * 0.044715) * (x * x))) + half


def _gelu_tanh_x2(x):
    c = math.sqrt(2.0 / math.pi)
    return x * jnp.tanh(x * (c + (c * 0.044715) * (x * x))) + x


def _softplus(z):
    return jnp.maximum(z, 0.0) + jnp.log1p(jnp.exp(-jnp.abs(z)))


def _dot(a, b):
    return jnp.dot(a.astype(BF16), b, preferred_element_type=F32)


def _gate_dots(xc, wrg_ref, brg_ref):
    out = []
    for gate in range(2):
        parts = [_dot(xc[:, j * MXU_DIM:(j + 1) * MXU_DIM], wrg_ref[gate, j])
                 for j in range(RG_TILES)]
        out.append(jnp.concatenate(parts, axis=1) + brg_ref[gate:gate + 1])
    return out


def _rg_coeffs(xc, r_half, i_half, lam):
    half_c = (-0.5 * RG_C) * _softplus(-lam)
    log_a = half_c * jnp.tanh(r_half) + half_c
    a = jnp.exp(log_a)
    y = jnp.tanh(log_a) * (-1.0 - a * a)
    mult = jnp.where(y > 0.0, y * lax.rsqrt(y), 0.0)
    i = 0.5 * jnp.tanh(i_half) + 0.5
    return a, mult * i * xc


def _pool_select(p2, p4, p8, p16):
    lane = lax.broadcasted_iota(jnp.int32, p2.shape, 1)
    return jnp.where(lane < POOL_GW, p2,
                     jnp.where(lane < 2 * POOL_GW, p4,
                               jnp.where(lane < 3 * POOL_GW, p8, p16)))


def _pool_window_lane(shape):
    lane = lax.broadcasted_iota(jnp.int32, shape, 1)
    return jnp.where(lane < POOL_GW, POOL_WINDOWS[0],
                     jnp.where(lane < 2 * POOL_GW, POOL_WINDOWS[1],
                               jnp.where(lane < 3 * POOL_GW, POOL_WINDOWS[2], POOL_WINDOWS[3])))


def _head_mask(shape, h):
    lane = lax.broadcasted_iota(jnp.int32, shape, 1)
    return (lane >= h * XA_HEAD_DIM) & (lane < (h + 1) * XA_HEAD_DIM)


def _softmax_rows(s):
    m = jnp.max(s, axis=-1, keepdims=True)
    e = jnp.exp(s - m)
    return e / jnp.sum(e, axis=-1, keepdims=True)


def _shift_rows(ext, k, ctx_rows):
    return pltpu.roll(ext, k, 0)[ctx_rows:]


def _scan_rows(a, b, h_in):
    n, c = a.shape
    groups = n // SUBLANES
    a3 = a.reshape(groups, SUBLANES, c)
    b3 = b.reshape(groups, SUBLANES, c)
    sub = lax.broadcasted_iota(jnp.int32, (1, SUBLANES, c), 1)
    k = 1
    while k < SUBLANES:
        valid = sub >= k
        a_prev = jnp.where(valid, pltpu.roll(a3, k, 1), 1.0)
        b_prev = jnp.where(valid, pltpu.roll(b3, k, 1), 0.0)
        b3 = a3 * b_prev + b3
        a3 = a3 * a_prev
        k *= 2
    carry = h_in
    hs = []
    for g in range(groups):
        h = a3[g] * carry + b3[g]
        hs.append(h)
        carry = h[SUBLANES - 1:SUBLANES]
    return jnp.concatenate(hs, axis=0)


def _ffn_chunk(xn2, c, wfg_ref, wfu_ref, fcw_ref, fcb_ref, ffn_ctx, offn_ref):
    tt = xn2.shape[0]
    cols = slice(c * FF_CHUNK, (c + 1) * FF_CHUNK)
    gate = jnp.dot(xn2, wfg_ref[...], preferred_element_type=F32)
    up_half = jnp.dot(xn2, wfu_ref[...], preferred_element_type=F32)
    ext = jnp.concatenate([ffn_ctx[:, cols], gate], axis=0)
    fw = fcw_ref[:, cols]
    gate_c = fcb_ref[:, cols] + fw[FFN_CONV - 1:FFN_CONV] * gate
    for k in range(1, FFN_CONV):
        gate_c = gate_c + fw[FFN_CONV - 1 - k:FFN_CONV - k] * _shift_rows(ext, k, SUBLANES)
    new_ffn = ext[tt:]
    ffn_ctx[:, cols] = new_ffn
    offn_ref[0, :, cols] = new_ffn
    return (_gelu_tanh_x2(gate_c) * up_half).astype(BF16)


def _mem_kv_kernel(mem_ref, gmem_ref, wkv_ref, mk_ref, mv_ref, kbd_ref, vbd_ref):
    mn = _rms(mem_ref[0], gmem_ref[...])
    kv = _dot(mn, wkv_ref[...])
    kt = kv[:, :D_XA].T
    v = kv[:, D_XA:]
    mk_ref[0] = kt
    mv_ref[0] = v.T
    chan = lax.broadcasted_iota(jnp.int32, kt.shape, 0)
    for hd in range(XA_HEADS):
        own = (chan >= hd * XA_HEAD_DIM) & (chan < (hd + 1) * XA_HEAD_DIM)
        kbd_ref[0, :, hd * N_MEM:(hd + 1) * N_MEM] = jnp.where(own, kt, 0.0).astype(BF16)
        vbd_ref[0, hd * N_MEM:(hd + 1) * N_MEM, :] = (
            jnp.where(_head_mask(v.shape, hd), v, 0.0).astype(BF16))


def _mem_kv(mem, w):
    batch = mem.shape[0]
    per_seq = lambda shape: pl.BlockSpec((1,) + shape, lambda b: (b, 0, 0))
    return pl.pallas_call(
        _mem_kv_kernel,
        grid=(batch,),
        in_specs=[per_seq((N_MEM, D_MODEL)), _const_spec(w['g_mem'].shape),
                  _const_spec(w['w_kv'].shape)],
        out_specs=(per_seq((D_XA, N_MEM)), per_seq((D_XA, N_MEM)),
                   per_seq((D_XA, XA_HEADS * N_MEM)), per_seq((XA_HEADS * N_MEM, D_XA))),
        out_shape=(jax.ShapeDtypeStruct((batch, D_XA, N_MEM), F32),
                   jax.ShapeDtypeStruct((batch, D_XA, N_MEM), F32),
                   jax.ShapeDtypeStruct((batch, D_XA, XA_HEADS * N_MEM), BF16),
                   jax.ShapeDtypeStruct((batch, XA_HEADS * N_MEM, D_XA), BF16)),
        compiler_params=pltpu.CompilerParams(dimension_semantics=("arbitrary",)),
        name="mem_kv",
    )(mem, w['g_mem'], w['w_kv'])


def _prompt_kernel(*refs, nt, n_tiles):
    (x_ref, kbd_ref, vbd_ref, g1_ref, win_ref, wpool_ref, pscale_ref, cw_ref, cb_ref,
     wrg_ref, brg_ref, lam_ref, gout_ref, wout_ref, g2_ref, wfg_ref, wfu_ref,
     fcw_ref, fcb_ref, wfd_ref, gfin_ref,
     y_ref, opool_ref, oconv_ref, oh_ref, offn_ref,
     pool_ctx, conv_ctx, h_ctx, ffn_ctx, act_scr, x1_scr, xn2_scr) = refs
    wfg_refs = [wfg_ref.at[c] for c in range(FF_CHUNKS)]
    wfu_refs = [wfu_ref.at[c] for c in range(FF_CHUNKS)]
    s = pl.program_id(0)
    tm = lax.rem(jnp.minimum(s, n_tiles - 1), nt)
    tf = lax.rem(jnp.maximum(s - 1, 0), nt)
    tt = x_ref.shape[1]

    @pl.when(s == 0)
    def _no_previous_tile():
        x1_scr[...] = jnp.zeros_like(x1_scr)
        xn2_scr[...] = jnp.zeros_like(xn2_scr)

    @pl.when(tm == 0)
    def _start_sequence():
        pool_ctx[...] = jnp.zeros_like(pool_ctx)
        conv_ctx[...] = jnp.zeros_like(conv_ctx)
        h_ctx[...] = jnp.zeros_like(h_ctx)

    @pl.when(tf == 0)
    def _start_ffn_sequence():
        ffn_ctx[...] = jnp.zeros_like(ffn_ctx)

    xn2_prev = xn2_scr[...]
    def ffn(c):
        act_scr[:, c * FF_CHUNK:(c + 1) * FF_CHUNK] = _ffn_chunk(
            xn2_prev, c, wfg_refs[c], wfu_refs[c], fcw_ref, fcb_ref, ffn_ctx, offn_ref)

    def head_probs(hd):
        return _softmax_rows(scores[:, hd * N_MEM:(hd + 1) * N_MEM]).astype(BF16)

    ffn(0)
    x = x_ref[0]
    proj = _dot(_rms(x, g1_ref[...]), win_ref[...])
    u_pool = proj[:, :D_POOL]
    x_rnn = proj[:, D_POOL:D_POOL + D_RNN]
    g_rnn = proj[:, D_POOL + D_RNN:D_POOL + 2 * D_RNN]
    q = proj[:, D_POOL + 2 * D_RNN:]

    ext = jnp.concatenate([pool_ctx[...], u_pool], axis=0)
    p2 = ext + pltpu.roll(ext, 1, 0)
    p4 = p2 + pltpu.roll(p2, 2, 0)
    p8 = p4 + pltpu.roll(p4, 4, 0)
    p16 = p8 + pltpu.roll(p8, 8, 0)
    sm = _pool_select(p2, p4, p8, p16)[POOL_CTX_ROWS:]
    head = (POOL_CTX_ROWS, D_POOL)
    win = _pool_window_lane(head)
    pos = tm * tt + lax.broadcasted_iota(jnp.int32, head, 0)
    inv_head = 1.0 / jnp.minimum(pos + 1, win).astype(F32)
    inv_rest = 1.0 / win[:1].astype(F32)
    diff = jnp.concatenate([sm[:POOL_CTX_ROWS] * inv_head, sm[POOL_CTX_ROWS:] * inv_rest],
                           axis=0) - u_pool
    o_pool = _dot(diff, wpool_ref[...]) * pscale_ref[...]
    new_pool = ext[tt:]
    pool_ctx[...] = new_pool

    ext = jnp.concatenate([conv_ctx[...], x_rnn], axis=0)
    cw = cw_ref[...]
    xc = cb_ref[...] + cw[RNN_CONV - 1:RNN_CONV] * x_rnn
    for k in range(1, RNN_CONV):
        xc = xc + cw[RNN_CONV - 1 - k:RNN_CONV - k] * _shift_rows(ext, k, SUBLANES)
    new_conv = ext[tt:]
    conv_ctx[...] = new_conv

    ffn(1)
    r_half, i_half = _gate_dots(xc, wrg_ref, brg_ref)
    a, bt = _rg_coeffs(xc, r_half, i_half, lam_ref[...])
    scores = _dot(q * (1.0 / math.sqrt(XA_HEAD_DIM)), kbd_ref[0])

    probs = []
    for hd in range(XA_HEADS):
        ffn(2 + hd)
        probs.append(head_probs(hd))
    o_xa = jnp.dot(jnp.concatenate(probs, axis=1), vbd_ref[0], preferred_element_type=F32)

    x2 = x1_scr[...] + jnp.dot(act_scr[...], wfd_ref[...], preferred_element_type=F32)
    h = _scan_rows(a, bt, h_ctx[...])
    h_last = h[tt - 1:tt]
    h_ctx[...] = h_last
    o_rnn = _gelu_tanh(g_rnn) * h
    mix = jnp.concatenate([_group_rms(o_pool), _group_rms(o_rnn), _group_rms(o_xa)], axis=-1)

    x1 = x + _dot(mix * gout_ref[...], wout_ref[...])
    y_ref[0] = _rms(x2, gfin_ref[...])
    xn2 = _rms(x1, g2_ref[...]).astype(BF16)

    x1_scr[...] = x1
    xn2_scr[...] = xn2

    @pl.when(s < n_tiles)
    def _emit_mixer_state():
        opool_ref[0] = new_pool
        oconv_ref[0] = new_conv
        oh_ref[0] = h_last


def _const_spec(shape):
    zeros = (0,) * len(shape)
    return pl.BlockSpec(shape, lambda *_: zeros, pipeline_mode=pl.Buffered(1))


def _prompt_layer(x, kbd, vbd, w):
    batch, seq, _ = x.shape
    nt = seq // TOKEN_TILE
    n_tiles = batch * nt
    consts = [w['g_mix_norm'], w['w_in'], w['w_pool'], w['pool_scale'],
              w['rnn_conv_w'], w['rnn_conv_b'], w['w_rg'], w['b_rg'], w['rg_lambda'],
              w['g_mix_out'], w['w_out'], w['g_ffn_norm'], w['w_ff_gate'], w['w_ff_up_half'],
              w['ffn_conv_w'], w['ffn_conv_b'], w['w_ff_down'], w['g_final']]
    mixer_tile = lambda s: jnp.minimum(s, n_tiles - 1)
    ffn_tile = lambda s: jnp.maximum(s - 1, 0)
    mixer_seq = lambda s: (mixer_tile(s) // nt, 0, 0)
    ffn_seq = lambda s: (ffn_tile(s) // nt, 0, 0)
    in_specs = [pl.BlockSpec((1, TOKEN_TILE, D_MODEL),
                             lambda s: (mixer_tile(s) // nt, mixer_tile(s) % nt, 0)),
                pl.BlockSpec((1, D_XA, XA_HEADS * N_MEM), mixer_seq),
                pl.BlockSpec((1, XA_HEADS * N_MEM, D_XA), mixer_seq)]
    in_specs += [_const_spec(c.shape) for c in consts]
    out_shape = (jax.ShapeDtypeStruct((batch, seq, D_MODEL), F32),
                 jax.ShapeDtypeStruct((batch, POOL_CTX_ROWS, D_POOL), F32),
                 jax.ShapeDtypeStruct((batch, SUBLANES, D_RNN), F32),
                 jax.ShapeDtypeStruct((batch, 1, D_RNN), F32),
                 jax.ShapeDtypeStruct((batch, SUBLANES, D_FF), F32))
    out_specs = (pl.BlockSpec((1, TOKEN_TILE, D_MODEL),
                              lambda s: (ffn_tile(s) // nt, ffn_tile(s) % nt, 0)),
                 pl.BlockSpec((1, POOL_CTX_ROWS, D_POOL), mixer_seq),
                 pl.BlockSpec((1, SUBLANES, D_RNN), mixer_seq),
                 pl.BlockSpec((1, 1, D_RNN), mixer_seq),
                 pl.BlockSpec((1, SUBLANES, D_FF), ffn_seq))
    scratch = [pltpu.VMEM((POOL_CTX_ROWS, D_POOL), F32),
               pltpu.VMEM((SUBLANES, D_RNN), F32),
               pltpu.VMEM((1, D_RNN), F32),
               pltpu.VMEM((SUBLANES, D_FF), F32),
               pltpu.VMEM((TOKEN_TILE, D_FF), BF16),
               pltpu.VMEM((TOKEN_TILE, D_MODEL), F32),
               pltpu.VMEM((TOKEN_TILE, D_MODEL), BF16)]
    return pl.pallas_call(
        functools.partial(_prompt_kernel, nt=nt, n_tiles=n_tiles),
        grid=(n_tiles + 1,),
        in_specs=in_specs,
        out_specs=out_specs,
        out_shape=out_shape,
        scratch_shapes=scratch,
        compiler_params=pltpu.CompilerParams(
            dimension_semantics=("arbitrary",), vmem_limit_bytes=VMEM_LIMIT),
        name="prompt_layer",
    )(x, kbd, vbd, *consts)


def _sample_mixer_kernel(xs_ref, spool_ref, sconv_ref, sh_ref, ckt_ref, cvt_ref,
                         g1_ref, win_ref, wpool_ref, pscale_ref, cw_ref, cb_ref,
                         wrg_ref, brg_ref, lam_ref,
                         mix_ref, npool_ref, nconv_ref, nh_ref,
                         q_scr, o_scr):
    i = pl.program_id(0)
    nb, nt, _ = xs_ref.shape

    @pl.when(i == 0)
    def _dense():
        x = jnp.concatenate([xs_ref[:, t, :] for t in range(nt)], axis=0)
        proj = _dot(_rms(x, g1_ref[...]), win_ref[...])
        u_pool = proj[:, :D_POOL]
        x_rnn = proj[:, D_POOL:D_POOL + D_RNN]
        g_rnn = proj[:, D_POOL + D_RNN:D_POOL + 2 * D_RNN]
        q = proj[:, D_POOL + 2 * D_RNN:] * (1.0 / math.sqrt(XA_HEAD_DIM))
        rows = lambda v, t: v[t * nb:(t + 1) * nb]

        full = [spool_ref[j] for j in range(POOL_CTX)] + [rows(u_pool, t) for t in range(nt)]
        n = len(full)
        p2 = {j: full[j] + full[j - 1] for j in range(1, n)}
        p4 = {j: p2[j] + p2[j - 2] for j in range(3, n)}
        p8 = {j: p4[j] + p4[j - 4] for j in range(7, n)}
        p16 = {j: p8[j] + p8[j - 8] for j in range(15, n)}
        win = _pool_window_lane((nb, D_POOL))
        diffs = []
        for t in range(nt):
            j = POOL_CTX + t
            cnt = jnp.minimum(PAST_LEN + t + 1, win).astype(F32)
            diffs.append(_pool_select(p2[j], p4[j], p8[j], p16[j]) / cnt - full[j])
        o_pool = _dot(jnp.concatenate(diffs, axis=0), wpool_ref[...]) * pscale_ref[...]
        for j in range(POOL_CTX):
            npool_ref[j] = full[nt + j]

        fullc = [sconv_ref[j] for j in range(RNN_CONV - 1)] + [rows(x_rnn, t) for t in range(nt)]
        cw = cw_ref[...]
        xcs = []
        for t in range(nt):
            acc = cb_ref[...] + cw[0:1] * fullc[t]
            for k in range(1, RNN_CONV):
                acc = acc + cw[k:k + 1] * fullc[t + k]
            xcs.append(acc)
        for j in range(RNN_CONV - 1):
            nconv_ref[j] = fullc[nt + j]
        xc = jnp.concatenate(xcs, axis=0)
        r_half, i_half = _gate_dots(xc, wrg_ref, brg_ref)
        a, bt = _rg_coeffs(xc, r_half, i_half, lam_ref[...])
        h = sh_ref[...]
        hs = []
        for t in range(nt):
            h = rows(a, t) * h + rows(bt, t)
            hs.append(h)
        nh_ref[...] = h
        o_rnn = _gelu_tanh(g_rnn) * jnp.concatenate(hs, axis=0)

        mix_ref[:, :D_POOL] = _group_rms(o_pool)
        mix_ref[:, D_POOL:D_POOL + D_RNN] = _group_rms(o_rnn)
        q_scr[...] = jnp.zeros_like(q_scr)
        for t in range(nt):
            for half in range(XA_LANE_HALVES):
                q_scr[half, pl.ds(t, nb, stride=SUBLANES), :] = (
                    rows(q, t)[:, half * LANES:(half + 1) * LANES])

    for j in range(SEQ_CHUNK):
        seq_rows = pl.ds(pl.multiple_of((i * SEQ_CHUNK + j) * SUBLANES, SUBLANES), SUBLANES)
        qb = jnp.concatenate([q_scr[half, seq_rows, :] for half in range(XA_LANE_HALVES)],
                             axis=1)
        qbd = jnp.concatenate(
            [jnp.where(_head_mask(qb.shape, hd), qb, 0.0) for hd in range(XA_HEADS)], axis=0)
        ktb = ckt_ref[j].astype(BF16)
        vtb = cvt_ref[j].astype(BF16)
        p = _softmax_rows(jnp.dot(qbd.astype(BF16), ktb, preferred_element_type=F32))
        pv = lax.dot_general(p.astype(BF16), vtb, (((1,), (1,)), ((), ())),
                             preferred_element_type=F32)
        ob = jnp.zeros((SUBLANES, D_XA), F32)
        for hd in range(XA_HEADS):
            part = pv[hd * SUBLANES:(hd + 1) * SUBLANES]
            ob = jnp.where(_head_mask(part.shape, hd), part, ob)
        for half in range(XA_LANE_HALVES):
            o_scr[half, seq_rows, :] = ob[:, half * LANES:(half + 1) * LANES]

    @pl.when(i == pl.num_programs(0) - 1)
    def _finish():
        for t in range(nt):
            o_t = jnp.concatenate([o_scr[half, pl.ds(t, nb, stride=SUBLANES), :]
                                   for half in range(XA_LANE_HALVES)], axis=1)
            mix_ref[t * nb:(t + 1) * nb, D_POOL + D_RNN:] = _group_rms(o_t)


def _sample_mixer(xs, spool, sconv, sh, ckt, cvt, w):
    nb, nt, _ = xs.shape
    consts_in = [xs, spool, sconv, sh]
    consts_w = [w['g_mix_norm'], w['w_in'], w['w_pool'], w['pool_scale'], w['rnn_conv_w'],
                w['rnn_conv_b'], w['w_rg'], w['b_rg'], w['rg_lambda']]
    kv_spec = pl.BlockSpec((SEQ_CHUNK, D_XA, N_MEM), lambda i: (i, 0, 0))
    in_specs = ([_const_spec(c.shape) for c in consts_in] + [kv_spec, kv_spec]
                + [_const_spec(c.shape) for c in consts_w])
    out_shape = (jax.ShapeDtypeStruct((nt * nb, D_MODEL), F32),
                 jax.ShapeDtypeStruct(spool.shape, F32),
                 jax.ShapeDtypeStruct(sconv.shape, F32),
                 jax.ShapeDtypeStruct(sh.shape, F32))
    out_specs = tuple(pl.BlockSpec(s.shape, lambda i, n=len(s.shape): (0,) * n) for s in out_shape)
    return pl.pallas_call(
        _sample_mixer_kernel,
        grid=(nb // SEQ_CHUNK,),
        in_specs=in_specs,
        out_specs=out_specs,
        out_shape=out_shape,
        scratch_shapes=[pltpu.VMEM((XA_LANE_HALVES, nb * SUBLANES, LANES), F32),
                        pltpu.VMEM((XA_LANE_HALVES, nb * SUBLANES, LANES), F32)],
        compiler_params=pltpu.CompilerParams(
            dimension_semantics=("arbitrary",), vmem_limit_bytes=VMEM_LIMIT),
        name="sample_mixer",
    )(*consts_in, ckt, cvt, *consts_w)


def _sample_ffn_kernel(xs_ref, mix_ref, sffn_ref, gout_ref, wout_ref, g2_ref, wfg_ref, wfu_ref,
                       fcw_ref, fcb_ref, wfd_ref, gfin_ref, y_ref, nffn_ref,
                       x1_scr, xn2_scr, acc_scr):
    c = pl.program_id(0)
    nb, nt, _ = xs_ref.shape

    @pl.when(c == 0)
    def _out_projection():
        x = jnp.concatenate([xs_ref[:, t, :] for t in range(nt)], axis=0)
        x1 = x + _dot(mix_ref[...] * gout_ref[...], wout_ref[...])
        x1_scr[...] = x1
        xn2_scr[...] = _rms(x1, g2_ref[...]).astype(BF16)
        acc_scr[...] = jnp.zeros_like(acc_scr)

    xn2 = xn2_scr[...]
    gate = jnp.dot(xn2, wfg_ref[0], preferred_element_type=F32)
    up_half = jnp.dot(xn2, wfu_ref[0], preferred_element_type=F32)
    full = [sffn_ref[:, j, :] for j in range(FFN_CONV - 1)]
    full += [gate[t * nb:(t + 1) * nb] for t in range(nt)]
    fw = fcw_ref[...]
    convs = []
    for t in range(nt):
        a = fcb_ref[...] + fw[0:1] * full[t]
        for k in range(1, FFN_CONV):
            a = a + fw[k:k + 1] * full[t + k]
        convs.append(a)
    for j in range(FFN_CONV - 1):
        nffn_ref[:, j, :] = full[nt + j]
    act = (_gelu_tanh_x2(jnp.concatenate(convs, axis=0)) * up_half).astype(BF16)
    acc_scr[...] += jnp.dot(act, wfd_ref[...], preferred_element_type=F32)

    @pl.when(c == pl.num_programs(0) - 1)
    def _final_norm():
        y = _rms(x1_scr[...] + acc_scr[...], gfin_ref[...])
        for t in range(nt):
            y_ref[:, t, :] = y[t * nb:(t + 1) * nb]


def _sample_ffn(xs, mix, sffn, w):
    nb, taps, _ = sffn.shape
    rows = mix.shape[0]
    ins = [xs, mix, sffn, w['g_mix_out'], w['w_out'], w['g_ffn_norm'], w['w_ff_gate'],
           w['w_ff_up_half'], w['ffn_conv_w'], w['ffn_conv_b'], w['w_ff_down'], w['g_final']]
    chunk_cols = lambda c: (0, c)
    in_specs = [_const_spec(xs.shape), _const_spec(mix.shape),
                pl.BlockSpec((nb, taps, FF_CHUNK), lambda c: (0, 0, c)),
                _const_spec(w['g_mix_out'].shape), _const_spec(w['w_out'].shape),
                _const_spec(w['g_ffn_norm'].shape),
                pl.BlockSpec((1, D_MODEL, FF_CHUNK), lambda c: (c, 0, 0)),
                pl.BlockSpec((1, D_MODEL, FF_CHUNK), lambda c: (c, 0, 0)),
                pl.BlockSpec((FFN_CONV, FF_CHUNK), chunk_cols),
                pl.BlockSpec((1, FF_CHUNK), chunk_cols),
                pl.BlockSpec((FF_CHUNK, D_MODEL), lambda c: (c, 0)),
                _const_spec(w['g_final'].shape)]
    out_shape = (jax.ShapeDtypeStruct(xs.shape, F32), jax.ShapeDtypeStruct(sffn.shape, F32))
    out_specs = (pl.BlockSpec(xs.shape, lambda c: (0, 0, 0)),
                 pl.BlockSpec((nb, taps, FF_CHUNK), lambda c: (0, 0, c)))
    return pl.pallas_call(
        _sample_ffn_kernel,
        grid=(FF_CHUNKS,),
        in_specs=in_specs,
        out_specs=out_specs,
        out_shape=out_shape,
        scratch_shapes=[pltpu.VMEM((rows, D_MODEL), F32),
                        pltpu.VMEM((rows, D_MODEL), BF16),
                        pltpu.VMEM((rows, D_MODEL), F32)],
        compiler_params=pltpu.CompilerParams(
            dimension_semantics=("arbitrary",), vmem_limit_bytes=VMEM_LIMIT),
        name="sample_ffn",
    )(*ins)


def _cast_ffn_kernel(g_ref, u_ref, d_ref, go_ref, uo_ref, do_ref):
    go_ref[0] = g_ref[...].astype(BF16)
    uo_ref[0] = (0.5 * u_ref[...]).astype(BF16)
    do_ref[...] = d_ref[...].astype(BF16)


def _cast_ffn_weights(w_gate, w_up, w_down):
    col_in = pl.BlockSpec((D_MODEL, FF_CHUNK), lambda c: (0, c))
    row_blk = pl.BlockSpec((FF_CHUNK, D_MODEL), lambda c: (c, 0))
    slab_out = pl.BlockSpec((1, D_MODEL, FF_CHUNK), lambda c: (c, 0, 0))
    slabs = jax.ShapeDtypeStruct((FF_CHUNKS, D_MODEL, FF_CHUNK), BF16)
    return pl.pallas_call(
        _cast_ffn_kernel,
        grid=(FF_CHUNKS,),
        in_specs=[col_in, col_in, row_blk],
        out_specs=(slab_out, slab_out, row_blk),
        out_shape=(slabs, slabs, jax.ShapeDtypeStruct((D_FF, D_MODEL), BF16)),
        compiler_params=pltpu.CompilerParams(dimension_semantics=("arbitrary",)),
        name="cast_ffn_weights",
    )(w_gate, w_up, w_down)


def _cast_proj_kernel(win_ref, wout_ref, wk_ref, wv_ref, wino_ref, wouto_ref, wkvo_ref):
    wino_ref[...] = win_ref[...].astype(BF16)
    wouto_ref[...] = wout_ref[...].astype(BF16)
    wkvo_ref[:, :D_XA] = wk_ref[...].astype(BF16)
    wkvo_ref[:, D_XA:] = wv_ref[...].astype(BF16)


def _cast_proj_weights(w_in, w_out, w_mem_k, w_mem_v):
    rows = MXU_DIM
    blk = lambda n: pl.BlockSpec((rows, n), lambda r: (r, 0))
    return pl.pallas_call(
        _cast_proj_kernel,
        grid=(D_MODEL // rows,),
        in_specs=[blk(D_IN), blk(D_MODEL), blk(D_XA), blk(D_XA)],
        out_specs=(blk(D_IN), blk(D_MODEL), blk(2 * D_XA)),
        out_shape=(jax.ShapeDtypeStruct((D_MODEL, D_IN), BF16),
                   jax.ShapeDtypeStruct((D_MODEL, D_MODEL), BF16),
                   jax.ShapeDtypeStruct((D_MODEL, 2 * D_XA), BF16)),
        compiler_params=pltpu.CompilerParams(dimension_semantics=("arbitrary",)),
        name="cast_proj_weights",
    )(w_in, w_out, w_mem_k, w_mem_v)


def _block_diag(blocks):
    n, c, d = blocks.shape
    eye = jnp.eye(n, dtype=blocks.dtype)
    return (eye[:, None, :, None] * blocks[:, :, None, :]).reshape(n * c, n * d)


def _diag_tiles(blocks):
    bd = _block_diag(blocks)
    return jnp.stack([bd[j * MXU_DIM:(j + 1) * MXU_DIM, j * MXU_DIM:(j + 1) * MXU_DIM]
                      for j in range(bd.shape[0] // MXU_DIM)])


def _layer_weights(l, g_mix_norm, w_in, w_pool, pool_scale, rnn_conv_w, rnn_conv_b, w_rg_a,
                   b_rg_a, w_rg_x, b_rg_x, rg_lambda, g_mem_norm, w_mem_k, w_mem_v, g_mix_out,
                   w_out, g_ffn_norm, w_ff_gate, w_ff_up, ffn_conv_w, ffn_conv_b, w_ff_down,
                   g_final):
    row = lambda v: v.reshape(1, -1)
    wfg, wfu_half, wfd = _cast_ffn_weights(w_ff_gate[l], w_ff_up[l], w_ff_down[l])
    w_in_b, w_out_b, w_kv_b = _cast_proj_weights(w_in[l], w_out[l], w_mem_k[l], w_mem_v[l])
    return {
        'g_mix_norm': row(g_mix_norm[l]),
        'w_in': w_in_b,
        'w_pool': _block_diag(w_pool[l]).astype(BF16),
        'pool_scale': row(pool_scale[l]),
        'rnn_conv_w': rnn_conv_w[l],
        'rnn_conv_b': row(rnn_conv_b[l]),
        'w_rg': (0.5 * jnp.stack([_diag_tiles(w_rg_a[l]), _diag_tiles(w_rg_x[l])])).astype(BF16),
        'b_rg': 0.5 * jnp.stack([b_rg_a[l], b_rg_x[l]]),
        'rg_lambda': row(rg_lambda[l]),
        'g_mem': row(g_mem_norm[l]),
        'w_kv': w_kv_b,
        'g_mix_out': row(g_mix_out[l]),
        'w_out': w_out_b,
        'g_ffn_norm': row(g_ffn_norm[l]),
        'w_ff_gate': wfg,
        'w_ff_up_half': wfu_half,
        'ffn_conv_w': ffn_conv_w[l],
        'ffn_conv_b': row(ffn_conv_b[l]),
        'w_ff_down': wfd,
        'g_final': row(g_final),
    }


def _mem_major(cache):
    b = cache.shape[0]
    return jnp.transpose(cache, (0, 2, 3, 1)).reshape(b, D_XA, N_MEM)


def _mem_minor(kt):
    b = kt.shape[0]
    return jnp.transpose(kt.reshape(b, XA_HEADS, XA_HEAD_DIM, N_MEM), (0, 3, 1, 2))


def kernel(x_prompt, x_sample, mem_prompt, state_pool, state_rnn_conv, state_rnn_h, state_ffn_conv, cache_mem_k, cache_mem_v, g_mix_norm, w_in, w_pool, pool_scale, rnn_conv_w, rnn_conv_b, w_rg_a, b_rg_a, w_rg_x, b_rg_x, rg_lambda, g_mem_norm, w_mem_k, w_mem_v, g_mix_out, w_out, g_ffn_norm, w_ff_gate, w_ff_up, ffn_conv_w, ffn_conv_b, w_ff_down, g_final):
    depth = w_in.shape[0]
    assert depth == 1, "the final norm is fused into the single layer"
    bp = x_prompt.shape[0]
    w = _layer_weights(0, g_mix_norm, w_in, w_pool, pool_scale, rnn_conv_w, rnn_conv_b, w_rg_a,
                       b_rg_a, w_rg_x, b_rg_x, rg_lambda, g_mem_norm, w_mem_k, w_mem_v,
                       g_mix_out, w_out, g_ffn_norm, w_ff_gate, w_ff_up, ffn_conv_w, ffn_conv_b,
                       w_ff_down, g_final)

    mkt, mvt, kbd, vbd = _mem_kv(mem_prompt, w)
    y_p, pool_p, conv_p, h_p, ffn_p = _prompt_layer(x_prompt, kbd, vbd, w)

    spool = jnp.transpose(state_pool[0], (1, 0, 2))
    sconv = jnp.transpose(state_rnn_conv[0], (1, 0, 2))
    mix, pool_s, conv_s, h_s = _sample_mixer(x_sample, spool, sconv, state_rnn_h[0],
                                             _mem_major(cache_mem_k[0]), _mem_major(cache_mem_v[0]), w)
    y_s, ffn_s = _sample_ffn(x_sample, mix, state_ffn_conv[0], w)

    return (y_p,
            y_s,
            pool_p[:, POOL_CTX_ROWS - POOL_CTX:][None],
            conv_p[:, SUBLANES - (RNN_CONV - 1):][None],
            h_p.reshape(1, bp, D_RNN),
            ffn_p[:, SUBLANES - (FFN_CONV - 1):][None],
            _mem_minor(mkt)[None],
            _mem_minor(mvt)[None],
            jnp.transpose(pool_s, (1, 0, 2))[None],
            jnp.transpose(conv_s, (1, 0, 2))[None],
            h_s[None],
            ffn_s[None])
```

```python
import functools
import math

import jax
import jax.numpy as jnp
from jax import lax
from jax.experimental import pallas as pl
from jax.experimental.pallas import tpu as pltpu

D_MODEL = 1024
PAST_LEN = 16384
D_POOL = D_MODEL // 4
POOL_WINDOWS = (2, 4, 8, 16)
POOL_GW = D_POOL // len(POOL_WINDOWS)
POOL_CTX = max(POOL_WINDOWS) - 1
D_RNN = D_MODEL // 2
RG_BLOCKS = 8
RNN_CONV = 4
RG_C = 8.0
XA_HEADS = 4
XA_HEAD_DIM = 64
D_XA = XA_HEADS * XA_HEAD_DIM
D_IN = D_POOL + 2 * D_RNN + D_XA
N_MEM = 256
D_FF = 3 * D_MODEL
FFN_CONV = 3
EPS = 1e-6

SUBLANES = 8
LANES = 128
MXU_DIM = 256
XA_LANE_HALVES = D_XA // LANES
RG_TILES = D_RNN // MXU_DIM
POOL_CTX_ROWS = 16
TOKEN_TILE = 512
FF_CHUNK = 512
FF_CHUNKS = D_FF // FF_CHUNK
SEQ_CHUNK = 16
VMEM_LIMIT = 60 * 1024 * 1024

BF16 = jnp.bfloat16
F32 = jnp.float32


def _rms(x, g):
    ms = jnp.mean(x * x, axis=-1, keepdims=True)
    return x * lax.rsqrt(ms + EPS) * g


def _group_rms(x):
    ms = jnp.mean(x * x, axis=-1, keepdims=True)
    return x * lax.rsqrt(ms + EPS)


def _gelu_tanh(x):
    c = math.sqrt(2.0 / math.pi)
    half = 0.5 * x
    return half * jnp.tanh(x * (c + (c * 0.044715) * (x * x))) + half


def _gelu_tanh_x2(x):
    c = math.sqrt(2.0 / math.pi)
    return x * jnp.tanh(x * (c + (c * 0.044715) * (x * x))) + x


def _softplus(z):
    return jnp.maximum(z, 0.0) + jnp.log1p(jnp.exp(-jnp.abs(z)))


def _dot(a, b):
    return jnp.dot(a.astype(BF16), b, preferred_element_type=F32)


def _gate_dots(xc, wrg_ref, brg_ref):
    out = []
    for gate in range(2):
        parts = [_dot(xc[:, j * MXU_DIM:(j + 1) * MXU_DIM], wrg_ref[gate, j])
                 for j in range(RG_TILES)]
        out.append(jnp.concatenate(parts, axis=1) + brg_ref[gate:gate + 1])
    return out


def _rg_coeffs(xc, r_half, i_half, lam):
    half_c = (-0.5 * RG_C) * _softplus(-lam)
    log_a = half_c * jnp.tanh(r_half) + half_c
    a = jnp.exp(log_a)
    y = jnp.tanh(log_a) * (-1.0 - a * a)
    mult = jnp.where(y > 0.0, y * lax.rsqrt(y), 0.0)
    i = 0.5 * jnp.tanh(i_half) + 0.5
    return a, mult * i * xc


def _pool_select(p2, p4, p8, p16):
    lane = lax.broadcasted_iota(jnp.int32, p2.shape, 1)
    return jnp.where(lane < POOL_GW, p2,
                     jnp.where(lane < 2 * POOL_GW, p4,
                               jnp.where(lane < 3 * POOL_GW, p8, p16)))


def _pool_window_lane(shape):
    lane = lax.broadcasted_iota(jnp.int32, shape, 1)
    return jnp.where(lane < POOL_GW, POOL_WINDOWS[0],
                     jnp.where(lane < 2 * POOL_GW, POOL_WINDOWS[1],
                               jnp.where(lane < 3 * POOL_GW, POOL_WINDOWS[2], POOL_WINDOWS[3])))


def _head_mask(shape, h):
    lane = lax.broadcasted_iota(jnp.int32, shape, 1)
    return (lane >= h * XA_HEAD_DIM) & (lane < (h + 1) * XA_HEAD_DIM)


def _softmax_rows(s):
    m = jnp.max(s, axis=-1, keepdims=True)
    e = jnp.exp(s - m)
    return e / jnp.sum(e, axis=-1, keepdims=True)


def _shift_rows(ext, k, ctx_rows):
    return pltpu.roll(ext, k, 0)[ctx_rows:]


def _scan_rows(a, b, h_in):
    n, c = a.shape
    groups = n // SUBLANES
    a3 = a.reshape(groups, SUBLANES, c)
    b3 = b.reshape(groups, SUBLANES, c)
    sub = lax.broadcasted_iota(jnp.int32, (1, SUBLANES, c), 1)
    k = 1
    while k < SUBLANES:
        valid = sub >= k
        a_prev = jnp.where(valid, pltpu.roll(a3, k, 1), 1.0)
        b_prev = jnp.where(valid, pltpu.roll(b3, k, 1), 0.0)
        b3 = a3 * b_prev + b3
        a3 = a3 * a_prev
        k *= 2
    carry = h_in
    hs = []
    for g in range(groups):
        h = a3[g] * carry + b3[g]
        hs.append(h)
        carry = h[SUBLANES - 1:SUBLANES]
    return jnp.concatenate(hs, axis=0)


def _ffn_chunk(xn2, c, wfg_ref, wfu_ref, fcw_ref, fcb_ref, ffn_ctx, offn_ref):
    tt = xn2.shape[0]
    cols = slice(c * FF_CHUNK, (c + 1) * FF_CHUNK)
    gate = jnp.dot(xn2, wfg_ref[...], preferred_element_type=F32)
    up_half = jnp.dot(xn2, wfu_ref[...], preferred_element_type=F32)
    ext = jnp.concatenate([ffn_ctx[:, cols], gate], axis=0)
    fw = fcw_ref[:, cols]
    gate_c = fcb_ref[:, cols] + fw[FFN_CONV - 1:FFN_CONV] * gate
    for k in range(1, FFN_CONV):
        gate_c = gate_c + fw[FFN_CONV - 1 - k:FFN_CONV - k] * _shift_rows(ext, k, SUBLANES)
    new_ffn = ext[tt:]
    ffn_ctx[:, cols] = new_ffn
    offn_ref[0, :, cols] = new_ffn
    return (_gelu_tanh_x2(gate_c) * up_half).astype(BF16)


def _mem_kv_kernel(mem_ref, gmem_ref, wkv_ref, mk_ref, mv_ref, kbd_ref, vbd_ref):
    mn = _rms(mem_ref[0], gmem_ref[...])
    kv = _dot(mn, wkv_ref[...])
    kt = kv[:, :D_XA].T
    v = kv[:, D_XA:]
    mk_ref[0] = kt
    mv_ref[0] = v.T
    chan = lax.broadcasted_iota(jnp.int32, kt.shape, 0)
    for hd in range(XA_HEADS):
        own = (chan >= hd * XA_HEAD_DIM) & (chan < (hd + 1) * XA_HEAD_DIM)
        kbd_ref[0, :, hd * N_MEM:(hd + 1) * N_MEM] = jnp.where(own, kt, 0.0).astype(BF16)
        vbd_ref[0, hd * N_MEM:(hd + 1) * N_MEM, :] = (
            jnp.where(_head_mask(v.shape, hd), v, 0.0).astype(BF16))


def _mem_kv(mem, w):
    batch = mem.shape[0]
    per_seq = lambda shape: pl.BlockSpec((1,) + shape, lambda b: (b, 0, 0))
    return pl.pallas_call(
        _mem_kv_kernel,
        grid=(batch,),
        in_specs=[per_seq((N_MEM, D_MODEL)), _const_spec(w['g_mem'].shape),
                  _const_spec(w['w_kv'].shape)],
        out_specs=(per_seq((D_XA, N_MEM)), per_seq((D_XA, N_MEM)),
                   per_seq((D_XA, XA_HEADS * N_MEM)), per_seq((XA_HEADS * N_MEM, D_XA))),
        out_shape=(jax.ShapeDtypeStruct((batch, D_XA, N_MEM), F32),
                   jax.ShapeDtypeStruct((batch, D_XA, N_MEM), F32),
                   jax.ShapeDtypeStruct((batch, D_XA, XA_HEADS * N_MEM), BF16),
                   jax.ShapeDtypeStruct((batch, XA_HEADS * N_MEM, D_XA), BF16)),
        compiler_params=pltpu.CompilerParams(dimension_semantics=("arbitrary",)),
        name="mem_kv",
    )(mem, w['g_mem'], w['w_kv'])


def _prompt_kernel(*refs, nt, n_tiles):
    (x_ref, kbd_ref, vbd_ref, g1_ref, win_ref, wpool_ref, pscale_ref, cw_ref, cb_ref,
     wrg_ref, brg_ref, lam_ref, gout_ref, wout_ref, g2_ref, wfg_ref, wfu_ref,
     fcw_ref, fcb_ref, wfd_ref, gfin_ref,
     y_ref, opool_ref, oconv_ref, oh_ref, offn_ref,
     pool_ctx, conv_ctx, h_ctx, ffn_ctx, act_scr, x1_scr, xn2_scr) = refs
    wfg_refs = [wfg_ref.at[c] for c in range(FF_CHUNKS)]
    wfu_refs = [wfu_ref.at[c] for c in range(FF_CHUNKS)]
    s = pl.program_id(0)
    tm = lax.rem(jnp.minimum(s, n_tiles - 1), nt)
    tf = lax.rem(jnp.maximum(s - 1, 0), nt)
    tt = x_ref.shape[1]

    zero_rows = 2 * SUBLANES

    def _zero_rows(r, carry):
        rows = pl.ds(pl.multiple_of(r * zero_rows, zero_rows), zero_rows)
        x1_scr[rows, :] = jnp.zeros((zero_rows, D_MODEL), F32)
        xn2_scr[rows, :] = jnp.zeros((zero_rows, D_MODEL), BF16)
        return carry

    lax.fori_loop(0, jnp.where(s == 0, tt // zero_rows, 0), _zero_rows, 0)

    @pl.when(tm == 0)
    def _start_sequence():
        pool_ctx[...] = jnp.zeros_like(pool_ctx)
        conv_ctx[...] = jnp.zeros_like(conv_ctx)
        h_ctx[...] = jnp.zeros_like(h_ctx)

    @pl.when(tf == 0)
    def _start_ffn_sequence():
        ffn_ctx[...] = jnp.zeros_like(ffn_ctx)

    xn2_prev = xn2_scr[...]
    def ffn(c):
        act_scr[:, c * FF_CHUNK:(c + 1) * FF_CHUNK] = _ffn_chunk(
            xn2_prev, c, wfg_refs[c], wfu_refs[c], fcw_ref, fcb_ref, ffn_ctx, offn_ref)

    def head_probs(hd):
        return _softmax_rows(scores[:, hd * N_MEM:(hd + 1) * N_MEM]).astype(BF16)

    ffn(0)
    x = x_ref[0]
    proj = _dot(_rms(x, g1_ref[...]), win_ref[...])
    u_pool = proj[:, :D_POOL]
    x_rnn = proj[:, D_POOL:D_POOL + D_RNN]
    g_rnn = proj[:, D_POOL + D_RNN:D_POOL + 2 * D_RNN]
    q = proj[:, D_POOL + 2 * D_RNN:]

    ext = jnp.concatenate([pool_ctx[...], u_pool], axis=0)
    p2 = ext + pltpu.roll(ext, 1, 0)
    p4 = p2 + pltpu.roll(p2, 2, 0)
    p8 = p4 + pltpu.roll(p4, 4, 0)
    p16 = p8 + pltpu.roll(p8, 8, 0)
    sm = _pool_select(p2, p4, p8, p16)[POOL_CTX_ROWS:]
    head = (POOL_CTX_ROWS, D_POOL)
    win = _pool_window_lane(head)
    pos = tm * tt + lax.broadcasted_iota(jnp.int32, head, 0)
    inv_head = 1.0 / jnp.minimum(pos + 1, win).astype(F32)
    inv_rest = 1.0 / win[:1].astype(F32)
    diff = jnp.concatenate([sm[:POOL_CTX_ROWS] * inv_head, sm[POOL_CTX_ROWS:] * inv_rest],
                           axis=0) - u_pool
    o_pool = _dot(diff, wpool_ref[...]) * pscale_ref[...]
    new_pool = ext[tt:]
    pool_ctx[...] = new_pool

    ext = jnp.concatenate([conv_ctx[...], x_rnn], axis=0)
    cw = cw_ref[...]
    xc = cb_ref[...] + cw[RNN_CONV - 1:RNN_CONV] * x_rnn
    for k in range(1, RNN_CONV):
        xc = xc + cw[RNN_CONV - 1 - k:RNN_CONV - k] * _shift_rows(ext, k, SUBLANES)
    new_conv = ext[tt:]
    conv_ctx[...] = new_conv

    ffn(1)
    r_half, i_half = _gate_dots(xc, wrg_ref, brg_ref)
    a, bt = _rg_coeffs(xc, r_half, i_half, lam_ref[...])
    scores = _dot(q * (1.0 / math.sqrt(XA_HEAD_DIM)), kbd_ref[0])

    probs = []
    for hd in range(XA_HEADS):
        ffn(2 + hd)
        probs.append(head_probs(hd))
    o_xa = jnp.dot(jnp.concatenate(probs, axis=1), vbd_ref[0], preferred_element_type=F32)

    x2 = x1_scr[...] + jnp.dot(act_scr[...], wfd_ref[...], preferred_element_type=F32)
    h = _scan_rows(a, bt, h_ctx[...])
    h_last = h[tt - 1:tt]
    h_ctx[...] = h_last
    o_rnn = _gelu_tanh(g_rnn) * h
    mix = jnp.concatenate([_group_rms(o_pool), _group_rms(o_rnn), _group_rms(o_xa)], axis=-1)

    x1 = x + _dot(mix * gout_ref[...], wout_ref[...])
    y_ref[0] = _rms(x2, gfin_ref[...])
    xn2 = _rms(x1, g2_ref[...]).astype(BF16)

    x1_scr[...] = x1
    xn2_scr[...] = xn2

    @pl.when(s < n_tiles)
    def _emit_mixer_state():
        opool_ref[0] = new_pool
        oconv_ref[0] = new_conv
        oh_ref[0] = h_last


def _const_spec(shape):
    zeros = (0,) * len(shape)
    return pl.BlockSpec(shape, lambda *_: zeros, pipeline_mode=pl.Buffered(1))


def _prompt_layer(x, kbd, vbd, w):
    batch, seq, _ = x.shape
    nt = seq // TOKEN_TILE
    n_tiles = batch * nt
    consts = [w['g_mix_norm'], w['w_in'], w['w_pool'], w['pool_scale'],
              w['rnn_conv_w'], w['rnn_conv_b'], w['w_rg'], w['b_rg'], w['rg_lambda'],
              w['g_mix_out'], w['w_out'], w['g_ffn_norm'], w['w_ff_gate'], w['w_ff_up_half'],
              w['ffn_conv_w'], w['ffn_conv_b'], w['w_ff_down'], w['g_final']]
    mixer_tile = lambda s: jnp.minimum(s, n_tiles - 1)
    ffn_tile = lambda s: jnp.maximum(s - 1, 0)
    mixer_seq = lambda s: (mixer_tile(s) // nt, 0, 0)
    ffn_seq = lambda s: (ffn_tile(s) // nt, 0, 0)
    in_specs = [pl.BlockSpec((1, TOKEN_TILE, D_MODEL),
                             lambda s: (mixer_tile(s) // nt, mixer_tile(s) % nt, 0)),
                pl.BlockSpec((1, D_XA, XA_HEADS * N_MEM), mixer_seq),
                pl.BlockSpec((1, XA_HEADS * N_MEM, D_XA), mixer_seq)]
    in_specs += [_const_spec(c.shape) for c in consts]
    out_shape = (jax.ShapeDtypeStruct((batch, seq, D_MODEL), F32),
                 jax.ShapeDtypeStruct((batch, POOL_CTX_ROWS, D_POOL), F32),
                 jax.ShapeDtypeStruct((batch, SUBLANES, D_RNN), F32),
                 jax.ShapeDtypeStruct((batch, 1, D_RNN), F32),
                 jax.ShapeDtypeStruct((batch, SUBLANES, D_FF), F32))
    out_specs = (pl.BlockSpec((1, TOKEN_TILE, D_MODEL),
                              lambda s: (ffn_tile(s) // nt, ffn_tile(s) % nt, 0)),
                 pl.BlockSpec((1, POOL_CTX_ROWS, D_POOL), mixer_seq),
                 pl.BlockSpec((1, SUBLANES, D_RNN), mixer_seq),
                 pl.BlockSpec((1, 1, D_RNN), mixer_seq),
                 pl.BlockSpec((1, SUBLANES, D_FF), ffn_seq))
    scratch = [pltpu.VMEM((POOL_CTX_ROWS, D_POOL), F32),
               pltpu.VMEM((SUBLANES, D_RNN), F32),
               pltpu.VMEM((1, D_RNN), F32),
               pltpu.VMEM((SUBLANES, D_FF), F32),
               pltpu.VMEM((TOKEN_TILE, D_FF), BF16),
               pltpu.VMEM((TOKEN_TILE, D_MODEL), F32),
               pltpu.VMEM((TOKEN_TILE, D_MODEL), BF16)]
    return pl.pallas_call(
        functools.partial(_prompt_kernel, nt=nt, n_tiles=n_tiles),
        grid=(n_tiles + 1,),
        in_specs=in_specs,
        out_specs=out_specs,
        out_shape=out_shape,
        scratch_shapes=scratch,
        compiler_params=pltpu.CompilerParams(
            dimension_semantics=("arbitrary",), vmem_limit_bytes=VMEM_LIMIT),
        name="prompt_layer",
    )(x, kbd, vbd, *consts)


def _sample_mixer_kernel(xs_ref, spool_ref, sconv_ref, sh_ref, ckt_ref, cvt_ref,
                         g1_ref, win_ref, wpool_ref, pscale_ref, cw_ref, cb_ref,
                         wrg_ref, brg_ref, lam_ref,
                         mix_ref, npool_ref, nconv_ref, nh_ref,
                         q_scr, o_scr):
    i = pl.program_id(0)
    nb, nt, _ = xs_ref.shape

    @pl.when(i == 0)
    def _dense():
        x = jnp.concatenate([xs_ref[:, t, :] for t in range(nt)], axis=0)
        proj = _dot(_rms(x, g1_ref[...]), win_ref[...])
        u_pool = proj[:, :D_POOL]
        x_rnn = proj[:, D_POOL:D_POOL + D_RNN]
        g_rnn = proj[:, D_POOL + D_RNN:D_POOL + 2 * D_RNN]
        q = proj[:, D_POOL + 2 * D_RNN:] * (1.0 / math.sqrt(XA_HEAD_DIM))
        rows = lambda v, t: v[t * nb:(t + 1) * nb]

        full = [spool_ref[j] for j in range(POOL_CTX)] + [rows(u_pool, t) for t in range(nt)]
        n = len(full)
        p2 = {j: full[j] + full[j - 1] for j in range(1, n)}
        p4 = {j: p2[j] + p2[j - 2] for j in range(3, n)}
        p8 = {j: p4[j] + p4[j - 4] for j in range(7, n)}
        p16 = {j: p8[j] + p8[j - 8] for j in range(15, n)}
        win = _pool_window_lane((nb, D_POOL))
        diffs = []
        for t in range(nt):
            j = POOL_CTX + t
            cnt = jnp.minimum(PAST_LEN + t + 1, win).astype(F32)
            diffs.append(_pool_select(p2[j], p4[j], p8[j], p16[j]) / cnt - full[j])
        o_pool = _dot(jnp.concatenate(diffs, axis=0), wpool_ref[...]) * pscale_ref[...]
        for j in range(POOL_CTX):
            npool_ref[j] = full[nt + j]

        fullc = [sconv_ref[j] for j in range(RNN_CONV - 1)] + [rows(x_rnn, t) for t in range(nt)]
        cw = cw_ref[...]
        xcs = []
        for t in range(nt):
            acc = cb_ref[...] + cw[0:1] * fullc[t]
            for k in range(1, RNN_CONV):
                acc = acc + cw[k:k + 1] * fullc[t + k]
            xcs.append(acc)
        for j in range(RNN_CONV - 1):
            nconv_ref[j] = fullc[nt + j]
        xc = jnp.concatenate(xcs, axis=0)
        r_half, i_half = _gate_dots(xc, wrg_ref, brg_ref)
        a, bt = _rg_coeffs(xc, r_half, i_half, lam_ref[...])
        h = sh_ref[...]
        hs = []
        for t in range(nt):
            h = rows(a, t) * h + rows(bt, t)
            hs.append(h)
        nh_ref[...] = h
        o_rnn = _gelu_tanh(g_rnn) * jnp.concatenate(hs, axis=0)

        mix_ref[:, :D_POOL] = _group_rms(o_pool)
        mix_ref[:, D_POOL:D_POOL + D_RNN] = _group_rms(o_rnn)
        q_scr[...] = jnp.zeros_like(q_scr)
        for t in range(nt):
            for half in range(XA_LANE_HALVES):
                q_scr[half, pl.ds(t, nb, stride=SUBLANES), :] = (
                    rows(q, t)[:, half * LANES:(half + 1) * LANES])

    for j in range(SEQ_CHUNK):
        seq_rows = pl.ds(pl.multiple_of((i * SEQ_CHUNK + j) * SUBLANES, SUBLANES), SUBLANES)
        qb = jnp.concatenate([q_scr[half, seq_rows, :] for half in range(XA_LANE_HALVES)],
                             axis=1)
        qbd = jnp.concatenate(
            [jnp.where(_head_mask(qb.shape, hd), qb, 0.0) for hd in range(XA_HEADS)], axis=0)
        ktb = ckt_ref[j].astype(BF16)
        vtb = cvt_ref[j].astype(BF16)
        p = _softmax_rows(jnp.dot(qbd.astype(BF16), ktb, preferred_element_type=F32))
        pv = lax.dot_general(p.astype(BF16), vtb, (((1,), (1,)), ((), ())),
                             preferred_element_type=F32)
        ob = jnp.zeros((SUBLANES, D_XA), F32)
        for hd in range(XA_HEADS):
            part = pv[hd * SUBLANES:(hd + 1) * SUBLANES]
            ob = jnp.where(_head_mask(part.shape, hd), part, ob)
        for half in range(XA_LANE_HALVES):
            o_scr[half, seq_rows, :] = ob[:, half * LANES:(half + 1) * LANES]

    @pl.when(i == pl.num_programs(0) - 1)
    def _finish():
        for t in range(nt):
            o_t = jnp.concatenate([o_scr[half, pl.ds(t, nb, stride=SUBLANES), :]
                                   for half in range(XA_LANE_HALVES)], axis=1)
            mix_ref[t * nb:(t + 1) * nb, D_POOL + D_RNN:] = _group_rms(o_t)


def _sample_mixer(xs, spool, sconv, sh, ckt, cvt, w):
    nb, nt, _ = xs.shape
    consts_in = [xs, spool, sconv, sh]
    consts_w = [w['g_mix_norm'], w['w_in'], w['w_pool'], w['pool_scale'], w['rnn_conv_w'],
                w['rnn_conv_b'], w['w_rg'], w['b_rg'], w['rg_lambda']]
    kv_spec = pl.BlockSpec((SEQ_CHUNK, D_XA, N_MEM), lambda i: (i, 0, 0))
    in_specs = ([_const_spec(c.shape) for c in consts_in] + [kv_spec, kv_spec]
                + [_const_spec(c.shape) for c in consts_w])
    out_shape = (jax.ShapeDtypeStruct((nt * nb, D_MODEL), F32),
                 jax.ShapeDtypeStruct(spool.shape, F32),
                 jax.ShapeDtypeStruct(sconv.shape, F32),
                 jax.ShapeDtypeStruct(sh.shape, F32))
    out_specs = tuple(pl.BlockSpec(s.shape, lambda i, n=len(s.shape): (0,) * n) for s in out_shape)
    return pl.pallas_call(
        _sample_mixer_kernel,
        grid=(nb // SEQ_CHUNK,),
        in_specs=in_specs,
        out_specs=out_specs,
        out_shape=out_shape,
        scratch_shapes=[pltpu.VMEM((XA_LANE_HALVES, nb * SUBLANES, LANES), F32),
                        pltpu.VMEM((XA_LANE_HALVES, nb * SUBLANES, LANES), F32)],
        compiler_params=pltpu.CompilerParams(
            dimension_semantics=("arbitrary",), vmem_limit_bytes=VMEM_LIMIT),
        name="sample_mixer",
    )(*consts_in, ckt, cvt, *consts_w)


def _sample_ffn_kernel(xs_ref, mix_ref, sffn_ref, gout_ref, wout_ref, g2_ref, wfg_ref, wfu_ref,
                       fcw_ref, fcb_ref, wfd_ref, gfin_ref, y_ref, nffn_ref,
                       x1_scr, xn2_scr, acc_scr):
    c = pl.program_id(0)
    nb, nt, _ = xs_ref.shape

    @pl.when(c == 0)
    def _out_projection():
        x = jnp.concatenate([xs_ref[:, t, :] for t in range(nt)], axis=0)
        x1 = x + _dot(mix_ref[...] * gout_ref[...], wout_ref[...])
        x1_scr[...] = x1
        xn2_scr[...] = _rms(x1, g2_ref[...]).astype(BF16)
        acc_scr[...] = jnp.zeros_like(acc_scr)

    xn2 = xn2_scr[...]
    gate = jnp.dot(xn2, wfg_ref[0], preferred_element_type=F32)
    up_half = jnp.dot(xn2, wfu_ref[0], preferred_element_type=F32)
    full = [sffn_ref[:, j, :] for j in range(FFN_CONV - 1)]
    full += [gate[t * nb:(t + 1) * nb] for t in range(nt)]
    fw = fcw_ref[...]
    convs = []
    for t in range(nt):
        a = fcb_ref[...] + fw[0:1] * full[t]
        for k in range(1, FFN_CONV):
            a = a + fw[k:k + 1] * full[t + k]
        convs.append(a)
    for j in range(FFN_CONV - 1):
        nffn_ref[:, j, :] = full[nt + j]
    act = (_gelu_tanh_x2(jnp.concatenate(convs, axis=0)) * up_half).astype(BF16)
    acc_scr[...] += jnp.dot(act, wfd_ref[...], preferred_element_type=F32)

    @pl.when(c == pl.num_programs(0) - 1)
    def _final_norm():
        y = _rms(x1_scr[...] + acc_scr[...], gfin_ref[...])
        for t in range(nt):
            y_ref[:, t, :] = y[t * nb:(t + 1) * nb]


def _sample_ffn(xs, mix, sffn, w):
    nb, taps, _ = sffn.shape
    rows = mix.shape[0]
    ins = [xs, mix, sffn, w['g_mix_out'], w['w_out'], w['g_ffn_norm'], w['w_ff_gate'],
           w['w_ff_up_half'], w['ffn_conv_w'], w['ffn_conv_b'], w['w_ff_down'], w['g_final']]
    chunk_cols = lambda c: (0, c)
    in_specs = [_const_spec(xs.shape), _const_spec(mix.shape),
                pl.BlockSpec((nb, taps, FF_CHUNK), lambda c: (0, 0, c)),
                _const_spec(w['g_mix_out'].shape), _const_spec(w['w_out'].shape),
                _const_spec(w['g_ffn_norm'].shape),
                pl.BlockSpec((1, D_MODEL, FF_CHUNK), lambda c: (c, 0, 0)),
                pl.BlockSpec((1, D_MODEL, FF_CHUNK), lambda c: (c, 0, 0)),
                pl.BlockSpec((FFN_CONV, FF_CHUNK), chunk_cols),
                pl.BlockSpec((1, FF_CHUNK), chunk_cols),
                pl.BlockSpec((FF_CHUNK, D_MODEL), lambda c: (c, 0)),
                _const_spec(w['g_final'].shape)]
    out_shape = (jax.ShapeDtypeStruct(xs.shape, F32), jax.ShapeDtypeStruct(sffn.shape, F32))
    out_specs = (pl.BlockSpec(xs.shape, lambda c: (0, 0, 0)),
                 pl.BlockSpec((nb, taps, FF_CHUNK), lambda c: (0, 0, c)))
    return pl.pallas_call(
        _sample_ffn_kernel,
        grid=(FF_CHUNKS,),
        in_specs=in_specs,
        out_specs=out_specs,
        out_shape=out_shape,
        scratch_shapes=[pltpu.VMEM((rows, D_MODEL), F32),
                        pltpu.VMEM((rows, D_MODEL), BF16),
                        pltpu.VMEM((rows, D_MODEL), F32)],
        compiler_params=pltpu.CompilerParams(
            dimension_semantics=("arbitrary",), vmem_limit_bytes=VMEM_LIMIT),
        name="sample_ffn",
    )(*ins)


def _cast_ffn_kernel(g_ref, u_ref, d_ref, go_ref, uo_ref, do_ref):
    go_ref[0] = g_ref[...].astype(BF16)
    uo_ref[0] = (0.5 * u_ref[...]).astype(BF16)
    do_ref[...] = d_ref[...].astype(BF16)


def _cast_ffn_weights(w_gate, w_up, w_down):
    col_in = pl.BlockSpec((D_MODEL, FF_CHUNK), lambda c: (0, c))
    row_blk = pl.BlockSpec((FF_CHUNK, D_MODEL), lambda c: (c, 0))
    slab_out = pl.BlockSpec((1, D_MODEL, FF_CHUNK), lambda c: (c, 0, 0))
    slabs = jax.ShapeDtypeStruct((FF_CHUNKS, D_MODEL, FF_CHUNK), BF16)
    return pl.pallas_call(
        _cast_ffn_kernel,
        grid=(FF_CHUNKS,),
        in_specs=[col_in, col_in, row_blk],
        out_specs=(slab_out, slab_out, row_blk),
        out_shape=(slabs, slabs, jax.ShapeDtypeStruct((D_FF, D_MODEL), BF16)),
        compiler_params=pltpu.CompilerParams(dimension_semantics=("arbitrary",)),
        name="cast_ffn_weights",
    )(w_gate, w_up, w_down)


def _cast_proj_kernel(win_ref, wout_ref, wk_ref, wv_ref, wino_ref, wouto_ref, wkvo_ref):
    wino_ref[...] = win_ref[...].astype(BF16)
    wouto_ref[...] = wout_ref[...].astype(BF16)
    wkvo_ref[:, :D_XA] = wk_ref[...].astype(BF16)
    wkvo_ref[:, D_XA:] = wv_ref[...].astype(BF16)


def _cast_proj_weights(w_in, w_out, w_mem_k, w_mem_v):
    rows = MXU_DIM
    blk = lambda n: pl.BlockSpec((rows, n), lambda r: (r, 0))
    return pl.pallas_call(
        _cast_proj_kernel,
        grid=(D_MODEL // rows,),
        in_specs=[blk(D_IN), blk(D_MODEL), blk(D_XA), blk(D_XA)],
        out_specs=(blk(D_IN), blk(D_MODEL), blk(2 * D_XA)),
        out_shape=(jax.ShapeDtypeStruct((D_MODEL, D_IN), BF16),
                   jax.ShapeDtypeStruct((D_MODEL, D_MODEL), BF16),
                   jax.ShapeDtypeStruct((D_MODEL, 2 * D_XA), BF16)),
        compiler_params=pltpu.CompilerParams(dimension_semantics=("arbitrary",)),
        name="cast_proj_weights",
    )(w_in, w_out, w_mem_k, w_mem_v)


def _block_diag(blocks):
    n, c, d = blocks.shape
    eye = jnp.eye(n, dtype=blocks.dtype)
    return (eye[:, None, :, None] * blocks[:, :, None, :]).reshape(n * c, n * d)


def _diag_tiles(blocks):
    bd = _block_diag(blocks)
    return jnp.stack([bd[j * MXU_DIM:(j + 1) * MXU_DIM, j * MXU_DIM:(j + 1) * MXU_DIM]
                      for j in range(bd.shape[0] // MXU_DIM)])


def _layer_weights(l, g_mix_norm, w_in, w_pool, pool_scale, rnn_conv_w, rnn_conv_b, w_rg_a,
                   b_rg_a, w_rg_x, b_rg_x, rg_lambda, g_mem_norm, w_mem_k, w_mem_v, g_mix_out,
                   w_out, g_ffn_norm, w_ff_gate, w_ff_up, ffn_conv_w, ffn_conv_b, w_ff_down,
                   g_final):
    row = lambda v: v.reshape(1, -1)
    wfg, wfu_half, wfd = _cast_ffn_weights(w_ff_gate[l], w_ff_up[l], w_ff_down[l])
    w_in_b, w_out_b, w_kv_b = _cast_proj_weights(w_in[l], w_out[l], w_mem_k[l], w_mem_v[l])
    return {
        'g_mix_norm': row(g_mix_norm[l]),
        'w_in': w_in_b,
        'w_pool': _block_diag(w_pool[l]).astype(BF16),
        'pool_scale': row(pool_scale[l]),
        'rnn_conv_w': rnn_conv_w[l],
        'rnn_conv_b': row(rnn_conv_b[l]),
        'w_rg': (0.5 * jnp.stack([_diag_tiles(w_rg_a[l]), _diag_tiles(w_rg_x[l])])).astype(BF16),
        'b_rg': 0.5 * jnp.stack([b_rg_a[l], b_rg_x[l]]),
        'rg_lambda': row(rg_lambda[l]),
        'g_mem': row(g_mem_norm[l]),
        'w_kv': w_kv_b,
        'g_mix_out': row(g_mix_out[l]),
        'w_out': w_out_b,
        'g_ffn_norm': row(g_ffn_norm[l]),
        'w_ff_gate': wfg,
        'w_ff_up_half': wfu_half,
        'ffn_conv_w': ffn_conv_w[l],
        'ffn_conv_b': row(ffn_conv_b[l]),
        'w_ff_down': wfd,
        'g_final': row(g_final),
    }


def _mem_major(cache):
    b = cache.shape[0]
    return jnp.transpose(cache, (0, 2, 3, 1)).reshape(b, D_XA, N_MEM)


def _mem_minor(kt):
    b = kt.shape[0]
    return jnp.transpose(kt.reshape(b, XA_HEADS, XA_HEAD_DIM, N_MEM), (0, 3, 1, 2))


def kernel(x_prompt, x_sample, mem_prompt, state_pool, state_rnn_conv, state_rnn_h, state_ffn_conv, cache_mem_k, cache_mem_v, g_mix_norm, w_in, w_pool, pool_scale, rnn_conv_w, rnn_conv_b, w_rg_a, b_rg_a, w_rg_x, b_rg_x, rg_lambda, g_mem_norm, w_mem_k, w_mem_v, g_mix_out, w_out, g_ffn_norm, w_ff_gate, w_ff_up, ffn_conv_w, ffn_conv_b, w_ff_down, g_final):
    depth = w_in.shape[0]
    assert depth == 1, "the final norm is fused into the single layer"
    bp = x_prompt.shape[0]
    w = _layer_weights(0, g_mix_norm, w_in, w_pool, pool_scale, rnn_conv_w, rnn_conv_b, w_rg_a,
                       b_rg_a, w_rg_x, b_rg_x, rg_lambda, g_mem_norm, w_mem_k, w_mem_v,
                       g_mix_out, w_out, g_ffn_norm, w_ff_gate, w_ff_up, ffn_conv_w, ffn_conv_b,
                       w_ff_down, g_final)

    mkt, mvt, kbd, vbd = _mem_kv(mem_prompt, w)
    y_p, pool_p, conv_p, h_p, ffn_p = _prompt_layer(x_prompt, kbd, vbd, w)

    spool = jnp.transpose(state_pool[0], (1, 0, 2))
    sconv = jnp.transpose(state_rnn_conv[0], (1, 0, 2))
    mix, pool_s, conv_s, h_s = _sample_mixer(x_sample, spool, sconv, state_rnn_h[0],
                                             _mem_major(cache_mem_k[0]), _mem_major(cache_mem_v[0]), w)
    y_s, ffn_s = _sample_ffn(x_sample, mix, state_ffn_conv[0], w)

    return (y_p,
            y_s,
            pool_p[:, POOL_CTX_ROWS - POOL_CTX:][None],
            conv_p[:, SUBLANES - (RNN_CONV - 1):][None],
            h_p.reshape(1, bp, D_RNN),
            ffn_p[:, SUBLANES - (FFN_CONV - 1):][None],
            _mem_minor(mkt)[None],
            _mem_minor(mvt)[None],
            jnp.transpose(pool_s, (1, 0, 2))[None],
            jnp.transpose(conv_s, (1, 0, 2))[None],
            h_s[None],
            ffn_s[None])
```

```python
import functools
import math

import jax
import jax.numpy as jnp
from jax import lax
from jax.experimental import pallas as pl
from jax.experimental.pallas import tpu as pltpu

D_MODEL = 1024
PAST_LEN = 16384
D_POOL = D_MODEL // 4
POOL_WINDOWS = (2, 4, 8, 16)
POOL_GW = D_POOL // len(POOL_WINDOWS)
POOL_CTX = max(POOL_WINDOWS) - 1
D_RNN = D_MODEL // 2
RG_BLOCKS = 8
RNN_CONV = 4
RG_C = 8.0
XA_HEADS = 4
XA_HEAD_DIM = 64
D_XA = XA_HEADS * XA_HEAD_DIM
D_IN = D_POOL + 2 * D_RNN + D_XA
N_MEM = 256
D_FF = 3 * D_MODEL
FFN_CONV = 3
EPS = 1e-6

SUBLANES = 8
LANES = 128
MXU_DIM = 256
XA_LANE_HALVES = D_XA // LANES
RG_TILES = D_RNN // MXU_DIM
POOL_CTX_ROWS = 16
TOKEN_TILE = 512
FF_CHUNK = 512
FF_CHUNKS = D_FF // FF_CHUNK
FFN_SLOTS = 6
SEQ_CHUNK = 16
MEM_SEQS_PER_STEP = 4
VMEM_LIMIT = 60 * 1024 * 1024

BF16 = jnp.bfloat16
F32 = jnp.float32


def _rms(x, g):
    ms = jnp.mean(x * x, axis=-1, keepdims=True)
    return x * lax.rsqrt(ms + EPS) * g


def _group_rms(x):
    ms = jnp.mean(x * x, axis=-1, keepdims=True)
    return x * lax.rsqrt(ms + EPS)


def _gelu_tanh(x):
    c = math.sqrt(2.0 / math.pi)
    half = 0.5 * x
    return half * jnp.tanh(x * (c + (c * 0.044715) * (x * x))) + half


def _gelu_tanh_x2(x):
    c = math.sqrt(2.0 / math.pi)
    return x * jnp.tanh(x * (c + (c * 0.044715) * (x * x))) + x


def _softplus(z):
    return jnp.maximum(z, 0.0) + jnp.log1p(jnp.exp(-jnp.abs(z)))


def _dot(a, b):
    return jnp.dot(a.astype(BF16), b, preferred_element_type=F32)


def _gate_dots(xc, wrg_ref, brg_ref):
    out = []
    for gate in range(2):
        parts = [_dot(xc[:, j * MXU_DIM:(j + 1) * MXU_DIM], wrg_ref[gate, j])
                 for j in range(RG_TILES)]
        out.append(jnp.concatenate(parts, axis=1) + brg_ref[gate:gate + 1])
    return out


def _rg_coeffs(xc, r_half, i_half, lam):
    half_c = (-0.5 * RG_C) * _softplus(-lam)
    log_a = half_c * jnp.tanh(r_half) + half_c
    a = jnp.exp(log_a)
    y = jnp.tanh(log_a) * (-1.0 - a * a)
    mult = jnp.where(y > 0.0, y * lax.rsqrt(y), 0.0)
    i = 0.5 * jnp.tanh(i_half) + 0.5
    return a, mult * i * xc


def _pool_select(p2, p4, p8, p16):
    lane = lax.broadcasted_iota(jnp.int32, p2.shape, 1)
    return jnp.where(lane < POOL_GW, p2,
                     jnp.where(lane < 2 * POOL_GW, p4,
                               jnp.where(lane < 3 * POOL_GW, p8, p16)))


def _pool_window_lane(shape):
    lane = lax.broadcasted_iota(jnp.int32, shape, 1)
    return jnp.where(lane < POOL_GW, POOL_WINDOWS[0],
                     jnp.where(lane < 2 * POOL_GW, POOL_WINDOWS[1],
                               jnp.where(lane < 3 * POOL_GW, POOL_WINDOWS[2], POOL_WINDOWS[3])))


def _head_mask(shape, h):
    lane = lax.broadcasted_iota(jnp.int32, shape, 1)
    return (lane >= h * XA_HEAD_DIM) & (lane < (h + 1) * XA_HEAD_DIM)


def _softmax_rows(s):
    m = jnp.max(s, axis=-1, keepdims=True)
    e = jnp.exp(s - m)
    return e / jnp.sum(e, axis=-1, keepdims=True)


def _shift_rows(ext, k, ctx_rows):
    return pltpu.roll(ext, k, 0)[ctx_rows:]


def _scan_rows(a, b, h_in):
    n, c = a.shape
    groups = n // SUBLANES
    a3 = a.reshape(groups, SUBLANES, c)
    b3 = b.reshape(groups, SUBLANES, c)
    sub = lax.broadcasted_iota(jnp.int32, (1, SUBLANES, c), 1)
    k = 1
    while k < SUBLANES:
        valid = sub >= k
        a_prev = jnp.where(valid, pltpu.roll(a3, k, 1), 1.0)
        b_prev = jnp.where(valid, pltpu.roll(b3, k, 1), 0.0)
        b3 = a3 * b_prev + b3
        a3 = a3 * a_prev
        k *= 2
    carry = h_in
    hs = []
    for g in range(groups):
        h = a3[g] * carry + b3[g]
        hs.append(h)
        carry = h[SUBLANES - 1:SUBLANES]
    return jnp.concatenate(hs, axis=0)


def _ffn_chunk(xn2, c, wfg_ref, wfu_ref, fcw_ref, fcb_ref, ffn_ctx, offn_ref):
    tt = xn2.shape[0]
    cols = slice(c * FF_CHUNK, (c + 1) * FF_CHUNK)
    gate = jnp.dot(xn2, wfg_ref[...], preferred_element_type=F32)
    up_half = jnp.dot(xn2, wfu_ref[...], preferred_element_type=F32)
    ext = jnp.concatenate([ffn_ctx[:, cols], gate], axis=0)
    fw = fcw_ref[:, cols]
    gate_c = fcb_ref[:, cols] + fw[FFN_CONV - 1:FFN_CONV] * gate
    for k in range(1, FFN_CONV):
        gate_c = gate_c + fw[FFN_CONV - 1 - k:FFN_CONV - k] * _shift_rows(ext, k, SUBLANES)
    new_ffn = ext[tt:]
    ffn_ctx[:, cols] = new_ffn
    offn_ref[0, :, cols] = new_ffn
    return (_gelu_tanh_x2(gate_c) * up_half).astype(BF16)


def _mem_kv_kernel(mem_ref, gmem_ref, wkv_ref, mk_ref, mv_ref, kbd_ref, vbd_ref):
    nseq = mem_ref.shape[0]
    mn = _rms(mem_ref[...].reshape(nseq * N_MEM, D_MODEL), gmem_ref[...])
    kv_all = _dot(mn, wkv_ref[...])
    for b in range(nseq):
        kv = kv_all[b * N_MEM:(b + 1) * N_MEM]
        kt = kv[:, :D_XA].T
        v = kv[:, D_XA:]
        mk_ref[b] = kt
        mv_ref[b] = v.T
        chan = lax.broadcasted_iota(jnp.int32, kt.shape, 0)
        for hd in range(XA_HEADS):
            own = (chan >= hd * XA_HEAD_DIM) & (chan < (hd + 1) * XA_HEAD_DIM)
            kbd_ref[b, :, hd * N_MEM:(hd + 1) * N_MEM] = jnp.where(own, kt, 0.0).astype(BF16)
            vbd_ref[b, hd * N_MEM:(hd + 1) * N_MEM, :] = (
                jnp.where(_head_mask(v.shape, hd), v, 0.0).astype(BF16))


def _mem_kv(mem, w):
    batch = mem.shape[0]
    per_seq = lambda shape: pl.BlockSpec((MEM_SEQS_PER_STEP,) + shape, lambda b: (b, 0, 0))
    return pl.pallas_call(
        _mem_kv_kernel,
        grid=(batch // MEM_SEQS_PER_STEP,),
        in_specs=[per_seq((N_MEM, D_MODEL)), _const_spec(w['g_mem'].shape),
                  _const_spec(w['w_kv'].shape)],
        out_specs=(per_seq((D_XA, N_MEM)), per_seq((D_XA, N_MEM)),
                   per_seq((D_XA, XA_HEADS * N_MEM)), per_seq((XA_HEADS * N_MEM, D_XA))),
        out_shape=(jax.ShapeDtypeStruct((batch, D_XA, N_MEM), F32),
                   jax.ShapeDtypeStruct((batch, D_XA, N_MEM), F32),
                   jax.ShapeDtypeStruct((batch, D_XA, XA_HEADS * N_MEM), BF16),
                   jax.ShapeDtypeStruct((batch, XA_HEADS * N_MEM, D_XA), BF16)),
        compiler_params=pltpu.CompilerParams(
            dimension_semantics=("arbitrary",), vmem_limit_bytes=VMEM_LIMIT),
        name="mem_kv",
    )(mem, w['g_mem'], w['w_kv'])


def _prompt_kernel(*refs, nt, n_tiles):
    (x_ref, kbd_ref, vbd_ref, g1_ref, win_ref, wpool_ref, pscale_ref, cw_ref, cb_ref,
     wrg_ref, brg_ref, lam_ref, gout_ref, wout_ref, g2_ref, wfg_ref, wfu_ref,
     fcw_ref, fcb_ref, wfd_ref, gfin_ref,
     y_ref, opool_ref, oconv_ref, oh_ref, offn_ref,
     pool_ctx, conv_ctx, h_ctx, ffn_ctx, act_scr, x1_scr, xn2_scr) = refs
    wfg_refs = [wfg_ref.at[c] for c in range(FF_CHUNKS)]
    wfu_refs = [wfu_ref.at[c] for c in range(FF_CHUNKS)]
    s = pl.program_id(0)
    tm = lax.rem(jnp.minimum(s, n_tiles - 1), nt)
    tf = lax.rem(jnp.maximum(s - 1, 0), nt)
    tt = x_ref.shape[1]

    zero_rows = 2 * SUBLANES

    def _zero_rows(r, carry):
        rows = pl.ds(pl.multiple_of(r * zero_rows, zero_rows), zero_rows)
        x1_scr[rows, :] = jnp.zeros((zero_rows, D_MODEL), F32)
        xn2_scr[rows, :] = jnp.zeros((zero_rows, D_MODEL), BF16)
        return carry

    lax.fori_loop(0, jnp.where(s == 0, tt // zero_rows, 0), _zero_rows, 0)

    @pl.when(tm == 0)
    def _start_sequence():
        pool_ctx[...] = jnp.zeros_like(pool_ctx)
        conv_ctx[...] = jnp.zeros_like(conv_ctx)
        h_ctx[...] = jnp.zeros_like(h_ctx)

    @pl.when(tf == 0)
    def _start_ffn_sequence():
        ffn_ctx[...] = jnp.zeros_like(ffn_ctx)

    xn2_prev = xn2_scr[...]

    def ffn(slot):
        if slot % (FFN_SLOTS // FF_CHUNKS):
            return
        c = slot // (FFN_SLOTS // FF_CHUNKS)
        act_scr[:, c * FF_CHUNK:(c + 1) * FF_CHUNK] = _ffn_chunk(
            xn2_prev, c, wfg_refs[c], wfu_refs[c], fcw_ref, fcb_ref, ffn_ctx, offn_ref)

    def head_probs(hd):
        return _softmax_rows(scores[:, hd * N_MEM:(hd + 1) * N_MEM]).astype(BF16)

    ffn(0)
    x = x_ref[0]
    proj = _dot(_rms(x, g1_ref[...]), win_ref[...])
    u_pool = proj[:, :D_POOL]
    x_rnn = proj[:, D_POOL:D_POOL + D_RNN]
    g_rnn = proj[:, D_POOL + D_RNN:D_POOL + 2 * D_RNN]
    q = proj[:, D_POOL + 2 * D_RNN:]

    ext = jnp.concatenate([pool_ctx[...], u_pool], axis=0)
    p2 = ext + pltpu.roll(ext, 1, 0)
    p4 = p2 + pltpu.roll(p2, 2, 0)
    p8 = p4 + pltpu.roll(p4, 4, 0)
    p16 = p8 + pltpu.roll(p8, 8, 0)
    sm = _pool_select(p2, p4, p8, p16)[POOL_CTX_ROWS:]
    head = (POOL_CTX_ROWS, D_POOL)
    win = _pool_window_lane(head)
    pos = tm * tt + lax.broadcasted_iota(jnp.int32, head, 0)
    inv_head = 1.0 / jnp.minimum(pos + 1, win).astype(F32)
    inv_rest = 1.0 / win[:1].astype(F32)
    diff = jnp.concatenate([sm[:POOL_CTX_ROWS] * inv_head, sm[POOL_CTX_ROWS:] * inv_rest],
                           axis=0) - u_pool
    o_pool = _dot(diff, wpool_ref[...]) * pscale_ref[...]
    new_pool = ext[tt:]
    pool_ctx[...] = new_pool

    ext = jnp.concatenate([conv_ctx[...], x_rnn], axis=0)
    cw = cw_ref[...]
    xc = cb_ref[...] + cw[RNN_CONV - 1:RNN_CONV] * x_rnn
    for k in range(1, RNN_CONV):
        xc = xc + cw[RNN_CONV - 1 - k:RNN_CONV - k] * _shift_rows(ext, k, SUBLANES)
    new_conv = ext[tt:]
    conv_ctx[...] = new_conv

    ffn(1)
    r_half, i_half = _gate_dots(xc, wrg_ref, brg_ref)
    a, bt = _rg_coeffs(xc, r_half, i_half, lam_ref[...])
    scores = _dot(q * (1.0 / math.sqrt(XA_HEAD_DIM)), kbd_ref[0])

    probs = []
    for hd in range(XA_HEADS):
        ffn(2 + hd)
        probs.append(head_probs(hd))
    o_xa = jnp.dot(jnp.concatenate(probs, axis=1), vbd_ref[0], preferred_element_type=F32)

    x2 = x1_scr[...] + jnp.dot(act_scr[...], wfd_ref[...], preferred_element_type=F32)
    h = _scan_rows(a, bt, h_ctx[...])
    h_last = h[tt - 1:tt]
    h_ctx[...] = h_last
    o_rnn = _gelu_tanh(g_rnn) * h
    mix = jnp.concatenate([_group_rms(o_pool), _group_rms(o_rnn), _group_rms(o_xa)], axis=-1)

    x1 = x + _dot(mix * gout_ref[...], wout_ref[...])
    y_ref[0] = _rms(x2, gfin_ref[...])
    xn2 = _rms(x1, g2_ref[...]).astype(BF16)

    x1_scr[...] = x1
    xn2_scr[...] = xn2

    @pl.when(s < n_tiles)
    def _emit_mixer_state():
        opool_ref[0] = new_pool
        oconv_ref[0] = new_conv
        oh_ref[0] = h_last


def _const_spec(shape):
    zeros = (0,) * len(shape)
    return pl.BlockSpec(shape, lambda *_: zeros, pipeline_mode=pl.Buffered(1))


def _prompt_layer(x, kbd, vbd, w):
    batch, seq, _ = x.shape
    nt = seq // TOKEN_TILE
    n_tiles = batch * nt
    consts = [w['g_mix_norm'], w['w_in'], w['w_pool'], w['pool_scale'],
              w['rnn_conv_w'], w['rnn_conv_b'], w['w_rg'], w['b_rg'], w['rg_lambda'],
              w['g_mix_out'], w['w_out'], w['g_ffn_norm'], w['w_ff_gate'], w['w_ff_up_half'],
              w['ffn_conv_w'], w['ffn_conv_b'], w['w_ff_down'], w['g_final']]
    mixer_tile = lambda s: jnp.minimum(s, n_tiles - 1)
    ffn_tile = lambda s: jnp.maximum(s - 1, 0)
    mixer_seq = lambda s: (mixer_tile(s) // nt, 0, 0)
    ffn_seq = lambda s: (ffn_tile(s) // nt, 0, 0)
    in_specs = [pl.BlockSpec((1, TOKEN_TILE, D_MODEL),
                             lambda s: (mixer_tile(s) // nt, mixer_tile(s) % nt, 0)),
                pl.BlockSpec((1, D_XA, XA_HEADS * N_MEM), mixer_seq),
                pl.BlockSpec((1, XA_HEADS * N_MEM, D_XA), mixer_seq)]
    in_specs += [_const_spec(c.shape) for c in consts]
    out_shape = (jax.ShapeDtypeStruct((batch, seq, D_MODEL), F32),
                 jax.ShapeDtypeStruct((batch, POOL_CTX_ROWS, D_POOL), F32),
                 jax.ShapeDtypeStruct((batch, SUBLANES, D_RNN), F32),
                 jax.ShapeDtypeStruct((batch, 1, D_RNN), F32),
                 jax.ShapeDtypeStruct((batch, SUBLANES, D_FF), F32))
    out_specs = (pl.BlockSpec((1, TOKEN_TILE, D_MODEL),
                              lambda s: (ffn_tile(s) // nt, ffn_tile(s) % nt, 0)),
                 pl.BlockSpec((1, POOL_CTX_ROWS, D_POOL), mixer_seq),
                 pl.BlockSpec((1, SUBLANES, D_RNN), mixer_seq),
                 pl.BlockSpec((1, 1, D_RNN), mixer_seq),
                 pl.BlockSpec((1, SUBLANES, D_FF), ffn_seq))
    scratch = [pltpu.VMEM((POOL_CTX_ROWS, D_POOL), F32),
               pltpu.VMEM((SUBLANES, D_RNN), F32),
               pltpu.VMEM((1, D_RNN), F32),
               pltpu.VMEM((SUBLANES, D_FF), F32),
               pltpu.VMEM((TOKEN_TILE, D_FF), BF16),
               pltpu.VMEM((TOKEN_TILE, D_MODEL), F32),
               pltpu.VMEM((TOKEN_TILE, D_MODEL), BF16)]
    return pl.pallas_call(
        functools.partial(_prompt_kernel, nt=nt, n_tiles=n_tiles),
        grid=(n_tiles + 1,),
        in_specs=in_specs,
        out_specs=out_specs,
        out_shape=out_shape,
        scratch_shapes=scratch,
        compiler_params=pltpu.CompilerParams(
            dimension_semantics=("arbitrary",), vmem_limit_bytes=VMEM_LIMIT),
        name="prompt_layer",
    )(x, kbd, vbd, *consts)


def _sample_mixer_kernel(xs_ref, spool_ref, sconv_ref, sh_ref, ckt_ref, cvt_ref,
                         g1_ref, win_ref, wpool_ref, pscale_ref, cw_ref, cb_ref,
                         wrg_ref, brg_ref, lam_ref,
                         mix_ref, npool_ref, nconv_ref, nh_ref,
                         q_scr, o_scr):
    i = pl.program_id(0)
    nb, nt, _ = xs_ref.shape

    @pl.when(i == 0)
    def _dense():
        x = jnp.concatenate([xs_ref[:, t, :] for t in range(nt)], axis=0)
        proj = _dot(_rms(x, g1_ref[...]), win_ref[...])
        u_pool = proj[:, :D_POOL]
        x_rnn = proj[:, D_POOL:D_POOL + D_RNN]
        g_rnn = proj[:, D_POOL + D_RNN:D_POOL + 2 * D_RNN]
        q = proj[:, D_POOL + 2 * D_RNN:] * (1.0 / math.sqrt(XA_HEAD_DIM))
        rows = lambda v, t: v[t * nb:(t + 1) * nb]

        full = [spool_ref[j] for j in range(POOL_CTX)] + [rows(u_pool, t) for t in range(nt)]
        n = len(full)
        p2 = {j: full[j] + full[j - 1] for j in range(1, n)}
        p4 = {j: p2[j] + p2[j - 2] for j in range(3, n)}
        p8 = {j: p4[j] + p4[j - 4] for j in range(7, n)}
        p16 = {j: p8[j] + p8[j - 8] for j in range(15, n)}
        win = _pool_window_lane((nb, D_POOL))
        diffs = []
        for t in range(nt):
            j = POOL_CTX + t
            cnt = jnp.minimum(PAST_LEN + t + 1, win).astype(F32)
            diffs.append(_pool_select(p2[j], p4[j], p8[j], p16[j]) / cnt - full[j])
        o_pool = _dot(jnp.concatenate(diffs, axis=0), wpool_ref[...]) * pscale_ref[...]
        for j in range(POOL_CTX):
            npool_ref[j] = full[nt + j]

        fullc = [sconv_ref[j] for j in range(RNN_CONV - 1)] + [rows(x_rnn, t) for t in range(nt)]
        cw = cw_ref[...]
        xcs = []
        for t in range(nt):
            acc = cb_ref[...] + cw[0:1] * fullc[t]
            for k in range(1, RNN_CONV):
                acc = acc + cw[k:k + 1] * fullc[t + k]
            xcs.append(acc)
        for j in range(RNN_CONV - 1):
            nconv_ref[j] = fullc[nt + j]
        xc = jnp.concatenate(xcs, axis=0)
        r_half, i_half = _gate_dots(xc, wrg_ref, brg_ref)
        a, bt = _rg_coeffs(xc, r_half, i_half, lam_ref[...])
        h = sh_ref[...]
        hs = []
        for t in range(nt):
            h = rows(a, t) * h + rows(bt, t)
            hs.append(h)
        nh_ref[...] = h
        o_rnn = _gelu_tanh(g_rnn) * jnp.concatenate(hs, axis=0)

        mix_ref[:, :D_POOL] = _group_rms(o_pool)
        mix_ref[:, D_POOL:D_POOL + D_RNN] = _group_rms(o_rnn)
        q_scr[...] = jnp.zeros_like(q_scr)
        for t in range(nt):
            for half in range(XA_LANE_HALVES):
                q_scr[half, pl.ds(t, nb, stride=SUBLANES), :] = (
                    rows(q, t)[:, half * LANES:(half + 1) * LANES])

    for j in range(SEQ_CHUNK):
        seq_rows = pl.ds(pl.multiple_of((i * SEQ_CHUNK + j) * SUBLANES, SUBLANES), SUBLANES)
        qb = jnp.concatenate([q_scr[half, seq_rows, :] for half in range(XA_LANE_HALVES)],
                             axis=1)
        qbd = jnp.concatenate(
            [jnp.where(_head_mask(qb.shape, hd), qb, 0.0) for hd in range(XA_HEADS)], axis=0)
        ktb = ckt_ref[j].astype(BF16)
        vtb = cvt_ref[j].astype(BF16)
        p = _softmax_rows(jnp.dot(qbd.astype(BF16), ktb, preferred_element_type=F32))
        pv = lax.dot_general(p.astype(BF16), vtb, (((1,), (1,)), ((), ())),
                             preferred_element_type=F32)
        ob = jnp.zeros((SUBLANES, D_XA), F32)
        for hd in range(XA_HEADS):
            part = pv[hd * SUBLANES:(hd + 1) * SUBLANES]
            ob = jnp.where(_head_mask(part.shape, hd), part, ob)
        for half in range(XA_LANE_HALVES):
            o_scr[half, seq_rows, :] = ob[:, half * LANES:(half + 1) * LANES]

    @pl.when(i == pl.num_programs(0) - 1)
    def _finish():
        for t in range(nt):
            o_t = jnp.concatenate([o_scr[half, pl.ds(t, nb, stride=SUBLANES), :]
                                   for half in range(XA_LANE_HALVES)], axis=1)
            mix_ref[t * nb:(t + 1) * nb, D_POOL + D_RNN:] = _group_rms(o_t)


def _sample_mixer(xs, spool, sconv, sh, ckt, cvt, w):
    nb, nt, _ = xs.shape
    consts_in = [xs, spool, sconv, sh]
    consts_w = [w['g_mix_norm'], w['w_in'], w['w_pool'], w['pool_scale'], w['rnn_conv_w'],
                w['rnn_conv_b'], w['w_rg'], w['b_rg'], w['rg_lambda']]
    kv_spec = pl.BlockSpec((SEQ_CHUNK, D_XA, N_MEM), lambda i: (i, 0, 0))
    in_specs = ([_const_spec(c.shape) for c in consts_in] + [kv_spec, kv_spec]
                + [_const_spec(c.shape) for c in consts_w])
    out_shape = (jax.ShapeDtypeStruct((nt * nb, D_MODEL), F32),
                 jax.ShapeDtypeStruct(spool.shape, F32),
                 jax.ShapeDtypeStruct(sconv.shape, F32),
                 jax.ShapeDtypeStruct(sh.shape, F32))
    out_specs = tuple(pl.BlockSpec(s.shape, lambda i, n=len(s.shape): (0,) * n) for s in out_shape)
    return pl.pallas_call(
        _sample_mixer_kernel,
        grid=(nb // SEQ_CHUNK,),
        in_specs=in_specs,
        out_specs=out_specs,
        out_shape=out_shape,
        scratch_shapes=[pltpu.VMEM((XA_LANE_HALVES, nb * SUBLANES, LANES), F32),
                        pltpu.VMEM((XA_LANE_HALVES, nb * SUBLANES, LANES), F32)],
        compiler_params=pltpu.CompilerParams(
            dimension_semantics=("arbitrary",), vmem_limit_bytes=VMEM_LIMIT),
        name="sample_mixer",
    )(*consts_in, ckt, cvt, *consts_w)


def _sample_ffn_kernel(xs_ref, mix_ref, sffn_ref, gout_ref, wout_ref, g2_ref, wfg_ref, wfu_ref,
                       fcw_ref, fcb_ref, wfd_ref, gfin_ref, y_ref, nffn_ref,
                       x1_scr, xn2_scr, acc_scr):
    c = pl.program_id(0)
    nb, nt, _ = xs_ref.shape

    @pl.when(c == 0)
    def _out_projection():
        x = jnp.concatenate([xs_ref[:, t, :] for t in range(nt)], axis=0)
        x1 = x + _dot(mix_ref[...] * gout_ref[...], wout_ref[...])
        x1_scr[...] = x1
        xn2_scr[...] = _rms(x1, g2_ref[...]).astype(BF16)
        acc_scr[...] = jnp.zeros_like(acc_scr)

    xn2 = xn2_scr[...]
    gate = jnp.dot(xn2, wfg_ref[0], preferred_element_type=F32)
    up_half = jnp.dot(xn2, wfu_ref[0], preferred_element_type=F32)
    full = [sffn_ref[:, j, :] for j in range(FFN_CONV - 1)]
    full += [gate[t * nb:(t + 1) * nb] for t in range(nt)]
    fw = fcw_ref[...]
    convs = []
    for t in range(nt):
        a = fcb_ref[...] + fw[0:1] * full[t]
        for k in range(1, FFN_CONV):
            a = a + fw[k:k + 1] * full[t + k]
        convs.append(a)
    for j in range(FFN_CONV - 1):
        nffn_ref[:, j, :] = full[nt + j]
    act = (_gelu_tanh_x2(jnp.concatenate(convs, axis=0)) * up_half).astype(BF16)
    acc_scr[...] += jnp.dot(act, wfd_ref[...], preferred_element_type=F32)

    @pl.when(c == pl.num_programs(0) - 1)
    def _final_norm():
        y = _rms(x1_scr[...] + acc_scr[...], gfin_ref[...])
        for t in range(nt):
            y_ref[:, t, :] = y[t * nb:(t + 1) * nb]


def _sample_ffn(xs, mix, sffn, w):
    nb, taps, _ = sffn.shape
    rows = mix.shape[0]
    ins = [xs, mix, sffn, w['g_mix_out'], w['w_out'], w['g_ffn_norm'], w['w_ff_gate'],
           w['w_ff_up_half'], w['ffn_conv_w'], w['ffn_conv_b'], w['w_ff_down'], w['g_final']]
    chunk_cols = lambda c: (0, c)
    in_specs = [_const_spec(xs.shape), _const_spec(mix.shape),
                pl.BlockSpec((nb, taps, FF_CHUNK), lambda c: (0, 0, c)),
                _const_spec(w['g_mix_out'].shape), _const_spec(w['w_out'].shape),
                _const_spec(w['g_ffn_norm'].shape),
                pl.BlockSpec((1, D_MODEL, FF_CHUNK), lambda c: (c, 0, 0)),
                pl.BlockSpec((1, D_MODEL, FF_CHUNK), lambda c: (c, 0, 0)),
                pl.BlockSpec((FFN_CONV, FF_CHUNK), chunk_cols),
                pl.BlockSpec((1, FF_CHUNK), chunk_cols),
                pl.BlockSpec((FF_CHUNK, D_MODEL), lambda c: (c, 0)),
                _const_spec(w['g_final'].shape)]
    out_shape = (jax.ShapeDtypeStruct(xs.shape, F32), jax.ShapeDtypeStruct(sffn.shape, F32))
    out_specs = (pl.BlockSpec(xs.shape, lambda c: (0, 0, 0)),
                 pl.BlockSpec((nb, taps, FF_CHUNK), lambda c: (0, 0, c)))
    return pl.pallas_call(
        _sample_ffn_kernel,
        grid=(FF_CHUNKS,),
        in_specs=in_specs,
        out_specs=out_specs,
        out_shape=out_shape,
        scratch_shapes=[pltpu.VMEM((rows, D_MODEL), F32),
                        pltpu.VMEM((rows, D_MODEL), BF16),
                        pltpu.VMEM((rows, D_MODEL), F32)],
        compiler_params=pltpu.CompilerParams(
            dimension_semantics=("arbitrary",), vmem_limit_bytes=VMEM_LIMIT),
        name="sample_ffn",
    )(*ins)


def _cast_ffn_kernel(g_ref, u_ref, d_ref, go_ref, uo_ref, do_ref):
    go_ref[0] = g_ref[...].astype(BF16)
    uo_ref[0] = (0.5 * u_ref[...]).astype(BF16)
    do_ref[...] = d_ref[...].astype(BF16)


def _cast_ffn_weights(w_gate, w_up, w_down):
    col_in = pl.BlockSpec((D_MODEL, FF_CHUNK), lambda c: (0, c))
    row_blk = pl.BlockSpec((FF_CHUNK, D_MODEL), lambda c: (c, 0))
    slab_out = pl.BlockSpec((1, D_MODEL, FF_CHUNK), lambda c: (c, 0, 0))
    slabs = jax.ShapeDtypeStruct((FF_CHUNKS, D_MODEL, FF_CHUNK), BF16)
    return pl.pallas_call(
        _cast_ffn_kernel,
        grid=(FF_CHUNKS,),
        in_specs=[col_in, col_in, row_blk],
        out_specs=(slab_out, slab_out, row_blk),
        out_shape=(slabs, slabs, jax.ShapeDtypeStruct((D_FF, D_MODEL), BF16)),
        compiler_params=pltpu.CompilerParams(dimension_semantics=("arbitrary",)),
        name="cast_ffn_weights",
    )(w_gate, w_up, w_down)


def _cast_proj_kernel(win_ref, wout_ref, wk_ref, wv_ref, wino_ref, wouto_ref, wkvo_ref):
    wino_ref[...] = win_ref[...].astype(BF16)
    wouto_ref[...] = wout_ref[...].astype(BF16)
    wkvo_ref[:, :D_XA] = wk_ref[...].astype(BF16)
    wkvo_ref[:, D_XA:] = wv_ref[...].astype(BF16)


def _cast_proj_weights(w_in, w_out, w_mem_k, w_mem_v):
    rows = MXU_DIM
    blk = lambda n: pl.BlockSpec((rows, n), lambda r: (r, 0))
    return pl.pallas_call(
        _cast_proj_kernel,
        grid=(D_MODEL // rows,),
        in_specs=[blk(D_IN), blk(D_MODEL), blk(D_XA), blk(D_XA)],
        out_specs=(blk(D_IN), blk(D_MODEL), blk(2 * D_XA)),
        out_shape=(jax.ShapeDtypeStruct((D_MODEL, D_IN), BF16),
                   jax.ShapeDtypeStruct((D_MODEL, D_MODEL), BF16),
                   jax.ShapeDtypeStruct((D_MODEL, 2 * D_XA), BF16)),
        compiler_params=pltpu.CompilerParams(dimension_semantics=("arbitrary",)),
        name="cast_proj_weights",
    )(w_in, w_out, w_mem_k, w_mem_v)


def _block_diag(blocks):
    n, c, d = blocks.shape
    eye = jnp.eye(n, dtype=blocks.dtype)
    return (eye[:, None, :, None] * blocks[:, :, None, :]).reshape(n * c, n * d)


def _diag_tiles(blocks):
    bd = _block_diag(blocks)
    return jnp.stack([bd[j * MXU_DIM:(j + 1) * MXU_DIM, j * MXU_DIM:(j + 1) * MXU_DIM]
                      for j in range(bd.shape[0] // MXU_DIM)])


def _layer_weights(l, g_mix_norm, w_in, w_pool, pool_scale, rnn_conv_w, rnn_conv_b, w_rg_a,
                   b_rg_a, w_rg_x, b_rg_x, rg_lambda, g_mem_norm, w_mem_k, w_mem_v, g_mix_out,
                   w_out, g_ffn_norm, w_ff_gate, w_ff_up, ffn_conv_w, ffn_conv_b, w_ff_down,
                   g_final):
    row = lambda v: v.reshape(1, -1)
    wfg, wfu_half, wfd = _cast_ffn_weights(w_ff_gate[l], w_ff_up[l], w_ff_down[l])
    w_in_b, w_out_b, w_kv_b = _cast_proj_weights(w_in[l], w_out[l], w_mem_k[l], w_mem_v[l])
    return {
        'g_mix_norm': row(g_mix_norm[l]),
        'w_in': w_in_b,
        'w_pool': _block_diag(w_pool[l]).astype(BF16),
        'pool_scale': row(pool_scale[l]),
        'rnn_conv_w': rnn_conv_w[l],
        'rnn_conv_b': row(rnn_conv_b[l]),
        'w_rg': (0.5 * jnp.stack([_diag_tiles(w_rg_a[l]), _diag_tiles(w_rg_x[l])])).astype(BF16),
        'b_rg': 0.5 * jnp.stack([b_rg_a[l], b_rg_x[l]]),
        'rg_lambda': row(rg_lambda[l]),
        'g_mem': row(g_mem_norm[l]),
        'w_kv': w_kv_b,
        'g_mix_out': row(g_mix_out[l]),
        'w_out': w_out_b,
        'g_ffn_norm': row(g_ffn_norm[l]),
        'w_ff_gate': wfg,
        'w_ff_up_half': wfu_half,
        'ffn_conv_w': ffn_conv_w[l],
        'ffn_conv_b': row(ffn_conv_b[l]),
        'w_ff_down': wfd,
        'g_final': row(g_final),
    }


def _mem_major(cache):
    b = cache.shape[0]
    return jnp.transpose(cache, (0, 2, 3, 1)).reshape(b, D_XA, N_MEM)


def _mem_minor(kt):
    b = kt.shape[0]
    return jnp.transpose(kt.reshape(b, XA_HEADS, XA_HEAD_DIM, N_MEM), (0, 3, 1, 2))


def kernel(x_prompt, x_sample, mem_prompt, state_pool, state_rnn_conv, state_rnn_h, state_ffn_conv, cache_mem_k, cache_mem_v, g_mix_norm, w_in, w_pool, pool_scale, rnn_conv_w, rnn_conv_b, w_rg_a, b_rg_a, w_rg_x, b_rg_x, rg_lambda, g_mem_norm, w_mem_k, w_mem_v, g_mix_out, w_out, g_ffn_norm, w_ff_gate, w_ff_up, ffn_conv_w, ffn_conv_b, w_ff_down, g_final):
    depth = w_in.shape[0]
    assert depth == 1, "the final norm is fused into the single layer"
    bp = x_prompt.shape[0]
    w = _layer_weights(0, g_mix_norm, w_in, w_pool, pool_scale, rnn_conv_w, rnn_conv_b, w_rg_a,
                       b_rg_a, w_rg_x, b_rg_x, rg_lambda, g_mem_norm, w_mem_k, w_mem_v,
                       g_mix_out, w_out, g_ffn_norm, w_ff_gate, w_ff_up, ffn_conv_w, ffn_conv_b,
                       w_ff_down, g_final)

    mkt, mvt, kbd, vbd = _mem_kv(mem_prompt, w)
    y_p, pool_p, conv_p, h_p, ffn_p = _prompt_layer(x_prompt, kbd, vbd, w)

    spool = jnp.transpose(state_pool[0], (1, 0, 2))
    sconv = jnp.transpose(state_rnn_conv[0], (1, 0, 2))
    mix, pool_s, conv_s, h_s = _sample_mixer(x_sample, spool, sconv, state_rnn_h[0],
                                             _mem_major(cache_mem_k[0]), _mem_major(cache_mem_v[0]), w)
    y_s, ffn_s = _sample_ffn(x_sample, mix, state_ffn_conv[0], w)

    return (y_p,
            y_s,
            pool_p[:, POOL_CTX_ROWS - POOL_CTX:][None],
            conv_p[:, SUBLANES - (RNN_CONV - 1):][None],
            h_p.reshape(1, bp, D_RNN),
            ffn_p[:, SUBLANES - (FFN_CONV - 1):][None],
            _mem_minor(mkt)[None],
            _mem_minor(mvt)[None],
            jnp.transpose(pool_s, (1, 0, 2))[None],
            jnp.transpose(conv_s, (1, 0, 2))[None],
            h_s[None],
            ffn_s[None])
```
